```python
import math
import jax, jax.numpy as jnp
from jax import lax
import numpy as np

D_MODEL = 1024
BATCH = 2
SEQ = 8192
DEPTH = 1

HEAD_DIM = 64
DSA_HEADS = 8
DIFF_HEADS = 4
DIFF_DIM = 64
IDX_HEADS = 8
IDX_DIM = 64
TOPK_MAX = 256
D_FF = 4 * D_MODEL
ROPE_THETA = 10000.0
EPS = 1e-6
Q_BLOCK = 128
NEG = -1e30

DSA_W = DSA_HEADS * HEAD_DIM
DIFF_W = DIFF_HEADS * 2 * DIFF_DIM
MIX_W = DSA_W + DIFF_W
IN_SIZES = (DSA_W, DSA_W, DSA_W,
            IDX_HEADS * IDX_DIM, IDX_DIM, IDX_HEADS,
            DIFF_W, DIFF_W, DIFF_W)
IN_COLS = int(sum(IN_SIZES))
SPLIT_POINTS = tuple(int(v) for v in np.cumsum(IN_SIZES)[:-1])

kernel_name = "hybrid_dsa_diffattn_sandwich_block"

f32 = jnp.float32


def rmsnorm(x, g):
    xf = x.astype(f32)
    y = xf * lax.rsqrt(jnp.mean(xf * xf, axis=-1, keepdims=True) + EPS)
    return (y * g.astype(f32)).astype(x.dtype)


def rope_tables(seq, dim):
    inv = 1.0 / (ROPE_THETA ** (jnp.arange(0, dim, 2, dtype=f32) / dim))
    ang = jnp.arange(seq, dtype=f32)[:, None] * inv[None, :]
    return jnp.cos(ang), jnp.sin(ang)


def apply_rope(x, cos, sin):
    xf = x.astype(f32)
    x1, x2 = jnp.split(xf, 2, axis=-1)
    c = cos[None, :, None, :]
    s = sin[None, :, None, :]
    return jnp.concatenate([x1 * c - x2 * s, x2 * c + x1 * s], axis=-1).astype(x.dtype)


def dsa_attention(q, k, v, q_idx, k_idx, w_idx):
    B, S, H, Dh = q.shape
    topk = min(TOPK_MAX, S // 4)
    n_blocks = S // Q_BLOCK
    key_pos = jnp.arange(S)
    k_idx_f = k_idx.astype(f32)
    idx_scale = (IDX_HEADS ** -0.5) * (IDX_DIM ** -0.5)

    def block(i):
        start = i * Q_BLOCK
        qpos = start + jnp.arange(Q_BLOCK)
        qi = lax.dynamic_slice_in_dim(q_idx, start, Q_BLOCK, axis=1).astype(f32)
        wi = lax.dynamic_slice_in_dim(w_idx, start, Q_BLOCK, axis=1).astype(f32)
        logits = jnp.einsum('bqhd,bsd->bqhs', qi, k_idx_f)
        score = jnp.einsum('bqhs,bqh->bqs', jax.nn.relu(logits), wi) * idx_scale
        causal = key_pos[None, :] <= qpos[:, None]
        score = jnp.where(causal[None], score, -jnp.inf)
        _, sel = lax.top_k(score, topk)
        valid = sel <= qpos[None, :, None]
        k_sel = jax.vmap(lambda a, ib: a[ib])(k, sel).astype(f32)
        v_sel = jax.vmap(lambda a, ib: a[ib])(v, sel).astype(f32)
        qb = lax.dynamic_slice_in_dim(q, start, Q_BLOCK, axis=1).astype(f32)
        s = jnp.einsum('bqhd,bqkhd->bhqk', qb, k_sel) * (Dh ** -0.5)
        s = jnp.where(valid[:, None], s, NEG)
        p = jax.nn.softmax(s, axis=-1)
        o = jnp.einsum('bhqk,bqkhd->bqhd', p, v_sel)
        return o.astype(q.dtype)

    out = lax.map(block, jnp.arange(n_blocks))
    return out.transpose(1, 0, 2, 3, 4).reshape(B, S, H, Dh)


def diff_attention(q1, q2, k1, k2, v, lam):
    B, S, H, Dd = q1.shape
    n_blocks = S // Q_BLOCK
    key_pos = jnp.arange(S)
    k1f, k2f, vf = k1.astype(f32), k2.astype(f32), v.astype(f32)

    def block(i):
        start = i * Q_BLOCK
        qpos = start + jnp.arange(Q_BLOCK)
        causal = (key_pos[None, :] <= qpos[:, None])[None, None]

        def probs(qh, kf):
            qb = lax.dynamic_slice_in_dim(qh, start, Q_BLOCK, axis=1).astype(f32)
            s = jnp.einsum('bqhd,bshd->bhqs', qb, kf) * (Dd ** -0.5)
            return jax.nn.softmax(jnp.where(causal, s, NEG), axis=-1)

        p = probs(q1, k1f) - lam * probs(q2, k2f)
        o = jnp.einsum('bhqs,bshe->bqhe', p, vf)
        return o.astype(v.dtype)

    out = lax.map(block, jnp.arange(n_blocks))
    return out.transpose(1, 0, 2, 3, 4).reshape(B, S, H, 2 * Dd)


def setup_inputs(seed: int = 0) -> dict:
    key = jax.random.key(seed)
    ks = jax.random.split(key, 16)
    nrm = jax.random.normal
    x = nrm(ks[0], (BATCH, SEQ, D_MODEL), f32)
    w_in = nrm(ks[1], (DEPTH, D_MODEL, IN_COLS), f32) * D_MODEL ** -0.5
    w_out = nrm(ks[2], (DEPTH, MIX_W, D_MODEL), f32) * MIX_W ** -0.5
    w_up = nrm(ks[3], (DEPTH, D_MODEL, D_FF), f32) * D_MODEL ** -0.5
    w_down = nrm(ks[4], (DEPTH, D_FF, D_MODEL), f32) * D_FF ** -0.5
    g_pre_mix = 1.0 + 0.02 * nrm(ks[5], (DEPTH, D_MODEL), f32)
    g_post_mix = 1.0 + 0.02 * nrm(ks[6], (DEPTH, D_MODEL), f32)
    g_pre_mlp = 1.0 + 0.02 * nrm(ks[7], (DEPTH, D_MODEL), f32)
    g_post_mlp = 1.0 + 0.02 * nrm(ks[8], (DEPTH, D_MODEL), f32)
    g_diff_sub = 1.0 + 0.02 * nrm(ks[9], (DEPTH, 2 * DIFF_DIM), f32)
    lambda_q1 = 0.1 * nrm(ks[10], (DEPTH, DIFF_DIM), f32)
    lambda_k1 = 0.1 * nrm(ks[11], (DEPTH, DIFF_DIM), f32)
    lambda_q2 = 0.1 * nrm(ks[12], (DEPTH, DIFF_DIM), f32)
    lambda_k2 = 0.1 * nrm(ks[13], (DEPTH, DIFF_DIM), f32)
    return {"x": x, "w_in": w_in, "w_out": w_out, "w_up": w_up, "w_down": w_down,
            "g_pre_mix": g_pre_mix, "g_post_mix": g_post_mix,
            "g_pre_mlp": g_pre_mlp, "g_post_mlp": g_post_mlp,
            "g_diff_sub": g_diff_sub,
            "lambda_q1": lambda_q1, "lambda_k1": lambda_k1,
            "lambda_q2": lambda_q2, "lambda_k2": lambda_k2}


def reference(x, w_in, w_out, w_up, w_down, g_pre_mix, g_post_mix, g_pre_mlp,
              g_post_mlp, g_diff_sub, lambda_q1, lambda_k1, lambda_q2, lambda_k2):
    B, S, _ = x.shape
    cos, sin = rope_tables(S, HEAD_DIM)
    for layer in range(DEPTH):
        h = rmsnorm(x, g_pre_mix[layer])
        proj = jnp.einsum('bsd,dc->bsc', h, w_in[layer])
        (q_a, k_a, v_a, q_i, k_i, w_i, q_d, k_d, v_d) = jnp.split(proj, SPLIT_POINTS, axis=-1)

        q_a = apply_rope(q_a.reshape(B, S, DSA_HEADS, HEAD_DIM), cos, sin)
        k_a = apply_rope(k_a.reshape(B, S, DSA_HEADS, HEAD_DIM), cos, sin)
        v_a = v_a.reshape(B, S, DSA_HEADS, HEAD_DIM)
        q_i = apply_rope(q_i.reshape(B, S, IDX_HEADS, IDX_DIM), cos, sin)
        k_i = apply_rope(k_i.reshape(B, S, 1, IDX_DIM), cos, sin)[:, :, 0]
        o_a = dsa_attention(q_a, k_a, v_a, q_i, k_i, w_i)

        q_d = q_d.reshape(B, S, DIFF_HEADS, 2, DIFF_DIM)
        k_d = k_d.reshape(B, S, DIFF_HEADS, 2, DIFF_DIM)
        v_d = v_d.reshape(B, S, DIFF_HEADS, 2 * DIFF_DIM)
        q1 = apply_rope(q_d[:, :, :, 0], cos, sin)
        q2 = apply_rope(q_d[:, :, :, 1], cos, sin)
        k1 = apply_rope(k_d[:, :, :, 0], cos, sin)
        k2 = apply_rope(k_d[:, :, :, 1], cos, sin)
        lam_init = 0.8 - 0.6 * math.exp(-0.3 * layer)
        lam = (jnp.exp(jnp.sum(lambda_q1[layer].astype(f32) * lambda_k1[layer].astype(f32)))
               - jnp.exp(jnp.sum(lambda_q2[layer].astype(f32) * lambda_k2[layer].astype(f32)))
               + lam_init)
        o_d = diff_attention(q1, q2, k1, k2, v_d, lam)
        o_d = (rmsnorm(o_d, g_diff_sub[layer]).astype(f32) * (1.0 - lam_init)).astype(x.dtype)

        mix = jnp.concatenate([o_a.reshape(B, S, DSA_W), o_d.reshape(B, S, DIFF_W)], axis=-1)
        y = jnp.einsum('bsc,cd->bsd', mix, w_out[layer])
        x = x + rmsnorm(y, g_post_mix[layer])

        h = rmsnorm(x, g_pre_mlp[layer])
        u = jax.nn.relu(jnp.einsum('bsd,df->bsf', h, w_up[layer]))
        y = jnp.einsum('bsf,fd->bsd', u * u, w_down[layer])
        x = x + rmsnorm(y, g_post_mlp[layer])
    return x
```

```python
import functools
import math

import jax
import jax.numpy as jnp
import numpy as np
from jax import lax
from jax.experimental import pallas as pl
from jax.experimental.pallas import tpu as pltpu

f32 = jnp.float32
bf16 = jnp.bfloat16
i32 = jnp.int32

D_MODEL = 1024
HEAD_DIM = 64
DSA_HEADS = 8
DIFF_HEADS = 4
DIFF_DIM = 64
IDX_HEADS = 8
IDX_DIM = 64
TOPK_MAX = 256
D_FF = 4 * D_MODEL
ROPE_THETA = 10000.0
EPS = 1e-6
NEG = -1e30

DSA_W = DSA_HEADS * HEAD_DIM
DIFF_W = DIFF_HEADS * 2 * DIFF_DIM
IN_SIZES = (DSA_W, DSA_W, DSA_W, IDX_HEADS * IDX_DIM, IDX_DIM, IDX_HEADS, DIFF_W, DIFF_W, DIFF_W)
SPLIT_POINTS = tuple(int(v) for v in np.cumsum(IN_SIZES)[:-1])

LANES = 128
INT_MIN = -(2 ** 31)
MIB = 1024 * 1024

PROJ_TM = 512
ROPE_COLS = 5 * 512 + LANES
PLAIN_COLS = 2 * 512 + LANES

TQ = 256
TK = 256
SEL_UNIT = 512
ROW_HALF = 128

_NT = (((1,), (1,)), ((), ()))


def _rms(x):
    return x * lax.rsqrt(jnp.mean(x * x, axis=-1, keepdims=True) + EPS)


def _tile_lanes(a, reps):
    return a if reps == 1 else jnp.concatenate([a] * reps, axis=1)


def _proj_kernel(x_ref, g_ref, cs_ref, sn_ref, wr_ref, wn_ref,
                 qa_ref, ka_ref, qi_ref, qd_ref, kd_ref, ki_ref, va_ref, vd_ref, wi_ref):
    h = (_rms(x_ref[...]) * g_ref[...]).astype(bf16)
    cs = cs_ref[...]
    sn = sn_ref[...]
    tm = h.shape[0]

    def rope_store(col0, width, out_ref):
        y = jnp.dot(h, wr_ref[:, col0:col0 + width], preferred_element_type=f32)
        lane = lax.broadcasted_iota(i32, (tm, width), 1)
        first_half = (lane & (HEAD_DIM - 1)) < HEAD_DIM // 2
        partner = jnp.where(first_half,
                            pltpu.roll(y, width - HEAD_DIM // 2, 1),
                            pltpu.roll(y, HEAD_DIM // 2, 1))
        reps = width // LANES
        out = y * _tile_lanes(cs, reps) + partner * _tile_lanes(sn, reps)
        out_ref[...] = out.astype(out_ref.dtype)

    rope_store(0, 512, qa_ref)
    rope_store(512, 512, ka_ref)
    rope_store(1024, 512, qi_ref)
    rope_store(1536, 512, qd_ref)
    rope_store(2048, 512, kd_ref)
    rope_store(2560, LANES, ki_ref)
    va_ref[...] = jnp.dot(h, wn_ref[:, 0:512], preferred_element_type=f32).astype(bf16)
    vd_ref[...] = jnp.dot(h, wn_ref[:, 512:1024], preferred_element_type=f32).astype(bf16)
    idx_scale = (IDX_HEADS ** -0.5) * (IDX_DIM ** -0.5)
    wi_ref[...] = jnp.dot(h, wn_ref[:, 1024:1024 + LANES], preferred_element_type=f32) * idx_scale


def _proj(x2d, g, cs, sn, wr, wn, seq):
    m = x2d.shape[0]
    tm = PROJ_TM
    pos_blocks = seq // tm
    row = lambda i: (i, 0)
    const = lambda i: (0, 0)
    wide = jax.ShapeDtypeStruct((m, 512), bf16)
    out_shape = (wide, wide, wide, wide, wide,
                 jax.ShapeDtypeStruct((m, LANES), bf16),
                 wide, wide,
                 jax.ShapeDtypeStruct((m, LANES), f32))
    wide_spec = pl.BlockSpec((tm, 512), row)
    lane_spec = pl.BlockSpec((tm, LANES), row)
    return pl.pallas_call(
        _proj_kernel,
        grid=(m // tm,),
        in_specs=[
            pl.BlockSpec((tm, D_MODEL), row),
            pl.BlockSpec((1, D_MODEL), const),
            pl.BlockSpec((tm, LANES), lambda i: (i % pos_blocks, 0)),
            pl.BlockSpec((tm, LANES), lambda i: (i % pos_blocks, 0)),
            pl.BlockSpec((D_MODEL, ROPE_COLS), const),
            pl.BlockSpec((D_MODEL, PLAIN_COLS), const),
        ],
        out_specs=(wide_spec, wide_spec, wide_spec, wide_spec, wide_spec, lane_spec,
                   wide_spec, wide_spec, lane_spec),
        out_shape=out_shape,
        compiler_params=pltpu.CompilerParams(
            dimension_semantics=("arbitrary",), vmem_limit_bytes=48 * MIB),
        name="proj",
    )(x2d, g, cs, sn, wr, wn)


def _split_pairs(src_ref, dst_sc, n_pairs):
    rows = dst_sc.shape[1]
    low = lax.broadcasted_iota(i32, (rows, LANES), 1) < HEAD_DIM
    for p in range(n_pairs):
        pair = src_ref[0, :, p * LANES:(p + 1) * LANES]
        zero = jnp.zeros_like(pair)
        dst_sc[2 * p] = jnp.where(low, pair, zero)
        dst_sc[2 * p + 1] = jnp.where(low, zero, pair)


def _dsa_kernel(qi_ref, wi_ref, qa_ref, ki_ref, ka_ref, va_ref, o_ref,
                keys_sc, thr_sc, wb_sc, qzi_sc, qza_sc, m_sc, l_sc, acc_sc):
    q0 = pl.program_id(1) * TQ
    n_units = (q0 + TQ + SEL_UNIT - 1) // SEL_UNIT
    n_chunks = n_units * (SEL_UNIT // TK)

    _split_pairs(qi_ref, qzi_sc, IDX_HEADS // 2)
    _split_pairs(qa_ref, qza_sc, DSA_HEADS // 2)
    w = wi_ref[0]
    for h in range(IDX_HEADS):
        wb_sc[h] = jnp.broadcast_to(w[:, h:h + 1], (TQ, LANES))

    row_a = q0 + lax.broadcasted_iota(i32, (TQ, TK), 0)
    col_a = lax.broadcasted_iota(i32, (TQ, TK), 1)

    def score_body(c, carry):
        k0 = pl.multiple_of(c * TK, TK)
        kc = ki_ref[0, pl.ds(k0, TK), :]
        score = jnp.zeros((TQ, TK), f32)
        for h in range(IDX_HEADS):
            logit = lax.dot_general(qzi_sc[h], kc, _NT, preferred_element_type=f32)
            score = score + _tile_lanes(wb_sc[h], TK // LANES) * jnp.maximum(logit, 0.0)
        bits = pltpu.bitcast(score, i32)
        key = bits ^ ((bits >> 31) & jnp.int32(0x7FFFFFFF))
        key = jnp.where(col_a + k0 <= row_a, key, jnp.int32(INT_MIN))
        keys_sc[:, pl.ds(k0, TK)] = key
        return carry

    lax.fori_loop(0, n_chunks, score_body, 0)

    def count(r0, pred):
        def unit_body(u, acc):
            base = pl.multiple_of(u * SEL_UNIT, SEL_UNIT)
            for jb in range(SEL_UNIT // LANES):
                blk = keys_sc[r0:r0 + ROW_HALF, pl.ds(base + jb * LANES, LANES)]
                col = base + jb * LANES + lax.broadcasted_iota(i32, (ROW_HALF, LANES), 1)
                acc = acc + jnp.where(pred(blk, col), 1.0, 0.0).astype(f32)
            return acc
        acc = lax.fori_loop(0, n_units, unit_body, jnp.zeros((ROW_HALF, LANES), f32))
        return jnp.broadcast_to(jnp.sum(acc, axis=1, keepdims=True), (ROW_HALF, LANES))

    for r0 in range(0, TQ, ROW_HALF):
        rowpos = q0 + r0 + lax.broadcasted_iota(i32, (ROW_HALF, LANES), 0)
        k_sel = jnp.minimum(rowpos + 1, TOPK_MAX).astype(f32)

        def bit_body(b, t):
            cand = t + lax.shift_left(jnp.int32(1), jnp.int32(31) - b)
            cnt = count(r0, lambda blk, col: blk >= cand)
            return jnp.where(cnt >= k_sel, cand, t)

        thr = lax.fori_loop(0, 32, bit_body, jnp.full((ROW_HALF, LANES), INT_MIN, i32))
        n_ge = count(r0, lambda blk, col: blk >= thr)
        thr_sc[r0:r0 + ROW_HALF, :] = thr

        @pl.when(jnp.max(n_ge - k_sel) > 0.0)
        def _():
            n_gt = count(r0, lambda blk, col: blk > thr)
            need = k_sel - n_gt

            def idx_body(b, j):
                cand = j + lax.shift_left(jnp.int32(1), jnp.int32(30) - b)
                cnt = count(r0, lambda blk, col: (blk == thr) & (col < cand))
                return jnp.where(cnt < need, cand, j)

            j_last = lax.fori_loop(0, 31, idx_body, jnp.zeros((ROW_HALF, LANES), i32))

            def fix_body(u, carry):
                base = pl.multiple_of(u * SEL_UNIT, SEL_UNIT)
                for jb in range(SEL_UNIT // LANES):
                    sl = pl.ds(base + jb * LANES, LANES)
                    blk = keys_sc[r0:r0 + ROW_HALF, sl]
                    col = base + jb * LANES + lax.broadcasted_iota(i32, (ROW_HALF, LANES), 1)
                    drop = (blk == thr) & (col > j_last)
                    keys_sc[r0:r0 + ROW_HALF, sl] = jnp.where(drop, blk - 1, blk)
                return carry

            lax.fori_loop(0, n_units, fix_body, 0)

    for h in range(DSA_HEADS):
        m_sc[h] = jnp.full((TQ, 1), NEG, f32)
        l_sc[h] = jnp.zeros((TQ, 1), f32)
        acc_sc[h] = jnp.zeros((TQ, LANES), f32)

    def attn_body(c, carry):
        k0 = pl.multiple_of(c * TK, TK)
        sel = keys_sc[:, pl.ds(k0, TK)] >= _tile_lanes(thr_sc[...], TK // LANES)
        for h in range(DSA_HEADS):
            pair = slice((h // 2) * LANES, (h // 2 + 1) * LANES)
            kc = ka_ref[0, pl.ds(k0, TK), pair]
            vc = va_ref[0, pl.ds(k0, TK), pair]
            s = lax.dot_general(qza_sc[h], kc, _NT, preferred_element_type=f32)
            s = jnp.where(sel, s, NEG)
            m_old = m_sc[h]
            m_new = jnp.maximum(m_old, jnp.max(s, axis=1, keepdims=True))
            p = jnp.exp(s - m_new)
            alpha = jnp.exp(m_old - m_new)
            l_sc[h] = alpha * l_sc[h] + jnp.sum(p, axis=1, keepdims=True)
            acc_sc[h] = alpha * acc_sc[h] + jnp.dot(p.astype(bf16), vc, preferred_element_type=f32)
            m_sc[h] = m_new
        return carry

    lax.fori_loop(0, (q0 + TQ) // TK, attn_body, 0)

    low = lax.broadcasted_iota(i32, (TQ, LANES), 1) < HEAD_DIM
    for p in range(DSA_HEADS // 2):
        even = acc_sc[2 * p] / l_sc[2 * p]
        odd = acc_sc[2 * p + 1] / l_sc[2 * p + 1]
        o_ref[0, :, p * LANES:(p + 1) * LANES] = jnp.where(low, even, odd).astype(o_ref.dtype)


def _dsa(qi, wi, qa, ki, ka, va):
    b, s, _ = qa.shape
    qblk = lambda bi, i: (bi, i, 0)
    full = lambda bi, i: (bi, 0, 0)
    resident = dict(pipeline_mode=pl.Buffered(1))
    return pl.pallas_call(
        _dsa_kernel,
        grid=(b, s // TQ),
        in_specs=[
            pl.BlockSpec((1, TQ, 512), qblk),
            pl.BlockSpec((1, TQ, LANES), qblk),
            pl.BlockSpec((1, TQ, 512), qblk),
            pl.BlockSpec((1, s, LANES), full, **resident),
            pl.BlockSpec((1, s, 512), full, **resident),
            pl.BlockSpec((1, s, 512), full, **resident),
        ],
        out_specs=pl.BlockSpec((1, TQ, 512), qblk),
        out_shape=jax.ShapeDtypeStruct((b, s, 512), bf16),
        scratch_shapes=[
            pltpu.VMEM((TQ, s), i32),
            pltpu.VMEM((TQ, LANES), i32),
            pltpu.VMEM((IDX_HEADS, TQ, LANES), f32),
            pltpu.VMEM((IDX_HEADS, TQ, LANES), bf16),
            pltpu.VMEM((DSA_HEADS, TQ, LANES), bf16),
            pltpu.VMEM((DSA_HEADS, TQ, 1), f32),
            pltpu.VMEM((DSA_HEADS, TQ, 1), f32),
            pltpu.VMEM((DSA_HEADS, TQ, LANES), f32),
        ],
        compiler_params=pltpu.CompilerParams(
            dimension_semantics=("arbitrary", "arbitrary"), vmem_limit_bytes=56 * MIB),
        name="dsa",
    )(qi, wi, qa, ki, ka, va)


def _diff_kernel(lq1_ref, lk1_ref, lq2_ref, lk2_ref, g_ref, qd_ref, kd_ref, vd_ref, o_ref,
                 qz_sc, m_sc, l_sc, acc_sc, *, lam_init):
    i = pl.program_id(1)
    _split_pairs(qd_ref, qz_sc, DIFF_HEADS)
    n_sm = 2 * DIFF_HEADS
    for j in range(n_sm):
        m_sc[j] = jnp.full((TQ, 1), NEG, f32)
        l_sc[j] = jnp.zeros((TQ, 1), f32)
        acc_sc[j] = jnp.zeros((TQ, LANES), f32)

    def chunk(c, masked):
        k0 = pl.multiple_of(c * TK, TK)
        if masked:
            causal = (lax.broadcasted_iota(i32, (TQ, TK), 1)
                      <= lax.broadcasted_iota(i32, (TQ, TK), 0))
        for h in range(DIFF_HEADS):
            pair = slice(h * LANES, (h + 1) * LANES)
            kc = kd_ref[0, pl.ds(k0, TK), pair]
            vc = vd_ref[0, pl.ds(k0, TK), pair]
            for j in (2 * h, 2 * h + 1):
                s = lax.dot_general(qz_sc[j], kc, _NT, preferred_element_type=f32)
                if masked:
                    s = jnp.where(causal, s, NEG)
                m_old = m_sc[j]
                m_new = jnp.maximum(m_old, jnp.max(s, axis=1, keepdims=True))
                p = jnp.exp(s - m_new)
                alpha = jnp.exp(m_old - m_new)
                l_sc[j] = alpha * l_sc[j] + jnp.sum(p, axis=1, keepdims=True)
                acc_sc[j] = alpha * acc_sc[j] + jnp.dot(p.astype(bf16), vc,
                                                       preferred_element_type=f32)
                m_sc[j] = m_new

    def body(c, carry):
        chunk(c, False)
        return carry

    lax.fori_loop(0, i, body, 0)
    chunk(i, True)

    lam = (jnp.exp(jnp.sum(lq1_ref[...] * lk1_ref[...], axis=1, keepdims=True))
           - jnp.exp(jnp.sum(lq2_ref[...] * lk2_ref[...], axis=1, keepdims=True)) + lam_init)
    g = g_ref[...]
    for h in range(DIFF_HEADS):
        o = acc_sc[2 * h] / l_sc[2 * h] - lam * (acc_sc[2 * h + 1] / l_sc[2 * h + 1])
        o = _rms(o) * g * (1.0 - lam_init)
        o_ref[0, :, h * LANES:(h + 1) * LANES] = o.astype(o_ref.dtype)


def _diff(lq1, lk1, lq2, lk2, g, qd, kd, vd, lam_init):
    assert TQ == TK
    b, s, _ = qd.shape
    qblk = lambda bi, i: (bi, i, 0)
    full = lambda bi, i: (bi, 0, 0)
    const = lambda bi, i: (0, 0)
    resident = dict(pipeline_mode=pl.Buffered(1))
    n_sm = 2 * DIFF_HEADS
    vec = pl.BlockSpec((1, DIFF_DIM), const)
    return pl.pallas_call(
        functools.partial(_diff_kernel, lam_init=lam_init),
        grid=(b, s // TQ),
        in_specs=[
            vec, vec, vec, vec,
            pl.BlockSpec((1, 2 * DIFF_DIM), const),
            pl.BlockSpec((1, TQ, 512), qblk),
            pl.BlockSpec((1, s, 512), full, **resident),
            pl.BlockSpec((1, s, 512), full, **resident),
        ],
        out_specs=pl.BlockSpec((1, TQ, 512), qblk),
        out_shape=jax.ShapeDtypeStruct((b, s, 512), bf16),
        scratch_shapes=[
            pltpu.VMEM((n_sm, TQ, LANES), bf16),
            pltpu.VMEM((n_sm, TQ, 1), f32),
            pltpu.VMEM((n_sm, TQ, 1), f32),
            pltpu.VMEM((n_sm, TQ, LANES), f32),
        ],
        compiler_params=pltpu.CompilerParams(
            dimension_semantics=("arbitrary", "arbitrary"), vmem_limit_bytes=48 * MIB),
        name="diff",
    )(lq1, lk1, lq2, lk2, g, qd, kd, vd)


def _post_kernel(x_ref, oa_ref, od_ref, wo_ref, wu_ref, wd_ref, g1_ref, g2_ref, g3_ref, o_ref):
    y = (jnp.dot(oa_ref[...], wo_ref[0:DSA_W, :], preferred_element_type=f32)
         + jnp.dot(od_ref[...], wo_ref[DSA_W:DSA_W + DIFF_W, :], preferred_element_type=f32))
    x1 = x_ref[...] + _rms(y) * g1_ref[...]
    h = (_rms(x1) * g2_ref[...]).astype(bf16)
    ff_chunk = 1024
    acc = jnp.zeros_like(x1)
    for c in range(D_FF // ff_chunk):
        cols = slice(c * ff_chunk, (c + 1) * ff_chunk)
        u = jnp.maximum(jnp.dot(h, wu_ref[:, cols], preferred_element_type=f32), 0.0)
        acc = acc + jnp.dot((u * u).astype(bf16), wd_ref[cols, :], preferred_element_type=f32)
    o_ref[...] = x1 + _rms(acc) * g3_ref[...]


def _post(x2d, oa, od, wo, wu, wd, g1, g2, g3):
    m = x2d.shape[0]
    tm = PROJ_TM
    row = lambda i: (i, 0)
    const = lambda i: (0, 0)
    resident = dict(pipeline_mode=pl.Buffered(1))
    gspec = pl.BlockSpec((1, D_MODEL), const)
    return pl.pallas_call(
        _post_kernel,
        grid=(m // tm,),
        in_specs=[
            pl.BlockSpec((tm, D_MODEL), row),
            pl.BlockSpec((tm, DSA_W), row),
            pl.BlockSpec((tm, DIFF_W), row),
            pl.BlockSpec((DSA_W + DIFF_W, D_MODEL), const, **resident),
            pl.BlockSpec((D_MODEL, D_FF), const, **resident),
            pl.BlockSpec((D_FF, D_MODEL), const, **resident),
            gspec, gspec, gspec,
        ],
        out_specs=pl.BlockSpec((tm, D_MODEL), row),
        out_shape=jax.ShapeDtypeStruct((m, D_MODEL), f32),
        compiler_params=pltpu.CompilerParams(
            dimension_semantics=("arbitrary",), vmem_limit_bytes=56 * MIB),
        name="post",
    )(x2d, oa, od, wo, wu, wd, g1, g2, g3)


def _rope_tables(seq):
    inv = 1.0 / (ROPE_THETA ** (jnp.arange(0, HEAD_DIM, 2, dtype=f32) / HEAD_DIM))
    ang = jnp.arange(seq, dtype=f32)[:, None] * inv[None, :]
    cos, sin = jnp.cos(ang), jnp.sin(ang)
    cs = jnp.concatenate([cos, cos, cos, cos], axis=1)
    sn = jnp.concatenate([-sin, sin, -sin, sin], axis=1)
    return cs, sn


def _prep_w_in(w):
    q_a, k_a, v_a, q_i, k_i, w_i, q_d, k_d, v_d = jnp.split(w, SPLIT_POINTS, axis=1)
    scale = HEAD_DIM ** -0.5
    wr = jnp.concatenate([q_a * scale, k_a, q_i, q_d * scale, k_d, k_i, k_i], axis=1)
    pad = jnp.zeros((D_MODEL, LANES - IDX_HEADS), w.dtype)
    wn = jnp.concatenate([v_a, v_d, w_i, pad], axis=1)
    return wr.astype(bf16), wn.astype(bf16)


def kernel(x, w_in, w_out, w_up, w_down, g_pre_mix, g_post_mix, g_pre_mlp, g_post_mlp,
           g_diff_sub, lambda_q1, lambda_k1, lambda_q2, lambda_k2):
    b, s, d = x.shape
    depth = w_in.shape[0]
    assert d == D_MODEL and s % SEL_UNIT == 0 and s % PROJ_TM == 0 and min(TOPK_MAX, s // 4) == TOPK_MAX
    cs, sn = _rope_tables(s)
    x2d = x.reshape(b * s, d)
    for layer in range(depth):
        wr, wn = _prep_w_in(w_in[layer])
        qa, ka, qi, qd, kd, ki, va, vd, wi = _proj(
            x2d, g_pre_mix[layer][None, :], cs, sn, wr, wn, s)
        r3 = lambda a: a.reshape(b, s, a.shape[-1])
        oa = _dsa(r3(qi), r3(wi), r3(qa), r3(ki), r3(ka), r3(va))
        lam_init = 0.8 - 0.6 * math.exp(-0.3 * layer)
        od = _diff(lambda_q1[layer][None, :], lambda_k1[layer][None, :],
                   lambda_q2[layer][None, :], lambda_k2[layer][None, :],
                   g_diff_sub[layer][None, :], r3(qd), r3(kd), r3(vd), lam_init)
        x2d = _post(x2d, oa.reshape(b * s, DSA_W), od.reshape(b * s, DIFF_W),
                    w_out[layer].astype(bf16), w_up[layer].astype(bf16), w_down[layer].astype(bf16),
                    g_post_mix[layer][None, :], g_pre_mlp[layer][None, :], g_post_mlp[layer][None, :])
    return x2d.reshape(b, s, d)
```

```python
import functools
import math

import jax
import jax.numpy as jnp
import numpy as np
from jax import lax
from jax.experimental import pallas as pl
from jax.experimental.pallas import tpu as pltpu

f32 = jnp.float32
bf16 = jnp.bfloat16
i32 = jnp.int32

D_MODEL = 1024
HEAD_DIM = 64
DSA_HEADS = 8
DIFF_HEADS = 4
DIFF_DIM = 64
IDX_HEADS = 8
IDX_DIM = 64
TOPK_MAX = 256
D_FF = 4 * D_MODEL
ROPE_THETA = 10000.0
EPS = 1e-6
NEG = -1e30

DSA_W = DSA_HEADS * HEAD_DIM
DIFF_W = DIFF_HEADS * 2 * DIFF_DIM
IN_SIZES = (DSA_W, DSA_W, DSA_W, IDX_HEADS * IDX_DIM, IDX_DIM, IDX_HEADS, DIFF_W, DIFF_W, DIFF_W)
SPLIT_POINTS = tuple(int(v) for v in np.cumsum(IN_SIZES)[:-1])
HALF = HEAD_DIM // 2

LANES = 128
SUBLANES = 8
BF16_ROWS = 16
MIB = 1024 * 1024

GROUP = 512
PROJ_TM = 512
WI_ROWS = BF16_ROWS
T_ROWS = 5 * GROUP + WI_ROWS
N_COLS = 2 * GROUP + LANES
FF_CHUNK = 1024

TQ = 256
TKI = 256
TKA = 512
SUM_ROWS = BF16_ROWS
SM_SCALE =HEAD_DIM ** -0.5 * math.log2(math.e)

_NT = (((1,), (1,)), ((), ()))


def _rms(x):
    return x * lax.rsqrt(jnp.mean(x * x, axis=-1, keepdims=True) + EPS)


def _proj_kernel(x_ref, g_ref, cs_ref, sn_ref, cst_ref, snt_ref, wn_ref, wt_ref,
                 ka_ref, kd_ref, ki_ref, qat_ref, qit_ref, qdt_ref, vat_ref, vdt_ref, wit_ref):
    h = (_rms(x_ref[...]) * g_ref[...]).astype(bf16)
    tm = h.shape[0]

    cs = cs_ref[...]
    sn = sn_ref[...]

    def rope_rows(col0, width, out_ref):
        y = jnp.dot(h, wn_ref[:, col0:col0 + width], preferred_element_type=f32)
        lane = lax.broadcasted_iota(i32, (tm, width), 1)
        partner = jnp.where((lane & (HEAD_DIM - 1)) < HALF,
                            pltpu.roll(y, width - HALF, 1), pltpu.roll(y, HALF, 1))
        reps = width // LANES
        c = cs if reps == 1 else jnp.concatenate([cs] * reps, axis=1)
        s = sn if reps == 1 else jnp.concatenate([sn] * reps, axis=1)
        out_ref[...] = (y * c + partner * s).astype(out_ref.dtype)

    rope_rows(0, GROUP, ka_ref)
    rope_rows(GROUP, GROUP, kd_ref)
    rope_rows(2 * GROUP, LANES, ki_ref)

    yt = lax.dot_general(wt_ref[...], h, _NT, preferred_element_type=f32)
    ct = cst_ref[...]
    st = snt_ref[...]

    def rope_cols(row0, out_ref):
        for hd in range(GROUP // HEAD_DIM):
            r = row0 + hd * HEAD_DIM
            x1 = yt[r:r + HALF]
            x2 = yt[r + HALF:r + HEAD_DIM]
            out_ref[hd * HEAD_DIM:hd * HEAD_DIM + HALF, :] = (x1 * ct - x2 * st).astype(out_ref.dtype)
            out_ref[hd * HEAD_DIM + HALF:(hd + 1) * HEAD_DIM, :] = (x2 * ct + x1 * st).astype(out_ref.dtype)

    rope_cols(0, qat_ref)
    rope_cols(GROUP, qit_ref)
    rope_cols(2 * GROUP, qdt_ref)
    vat_ref[...] = yt[3 * GROUP:4 * GROUP].astype(bf16)
    vdt_ref[...] = yt[4 * GROUP:5 * GROUP].astype(bf16)
    idx_scale = (IDX_HEADS ** -0.5) * (IDX_DIM ** -0.5)
    wit_ref[...] = yt[5 * GROUP:5 * GROUP + WI_ROWS] * idx_scale


def _proj(x2d, g, cs, sn, cst, snt, wn, wt, seq):
    m = x2d.shape[0]
    tm = PROJ_TM
    pos_blocks = seq // tm
    row = lambda i: (i, 0)
    col = lambda i: (0, i)
    const = lambda i: (0, 0)
    rows_out = jax.ShapeDtypeStruct((m, GROUP), bf16)
    cols_out = jax.ShapeDtypeStruct((GROUP, m), bf16)
    out_shape = (rows_out, rows_out, jax.ShapeDtypeStruct((m, LANES), bf16),
                 cols_out, cols_out, cols_out, cols_out, cols_out,
                 jax.ShapeDtypeStruct((WI_ROWS, m), f32))
    rows_spec = pl.BlockSpec((tm, GROUP), row)
    cols_spec = pl.BlockSpec((GROUP, tm), col)
    return pl.pallas_call(
        _proj_kernel,
        grid=(m // tm,),
        in_specs=[
            pl.BlockSpec((tm, D_MODEL), row),
            pl.BlockSpec((1, D_MODEL), const),
            pl.BlockSpec((tm, LANES), lambda i: (i % pos_blocks, 0)),
            pl.BlockSpec((tm, LANES), lambda i: (i % pos_blocks, 0)),
            pl.BlockSpec((HALF, tm), lambda i: (0, i % pos_blocks)),
            pl.BlockSpec((HALF, tm), lambda i: (0, i % pos_blocks)),
            pl.BlockSpec((D_MODEL, N_COLS), const),
            pl.BlockSpec((T_ROWS, D_MODEL), const),
        ],
        out_specs=(rows_spec, rows_spec, pl.BlockSpec((tm, LANES), row),
                   cols_spec, cols_spec, cols_spec, cols_spec, cols_spec,
                   pl.BlockSpec((WI_ROWS, tm), col)),
        out_shape=out_shape,
        compiler_params=pltpu.CompilerParams(
            dimension_semantics=("arbitrary",), vmem_limit_bytes=56 * MIB),
        name="proj",
    )(x2d, g, cs, sn, cst, snt, wn, wt)


def _split_pairs_t(src_ref, dst_sc, n_pairs):
    zero = jnp.zeros((HEAD_DIM, dst_sc.shape[2]), dst_sc.dtype)
    for p in range(n_pairs):
        r = p * LANES
        dst_sc[2 * p, 0:HEAD_DIM, :] = src_ref[r:r + HEAD_DIM, :]
        dst_sc[2 * p, HEAD_DIM:LANES, :] = zero
        dst_sc[2 * p + 1, 0:HEAD_DIM, :] = zero
        dst_sc[2 * p + 1, HEAD_DIM:LANES, :] = src_ref[r + HEAD_DIM:r + LANES, :]


def _softmax_chunk(n, score_fn, value_fn, s_sc, p_sc, a_sc, m_sc, acc_sc):
    ones = jnp.ones((SUM_ROWS, TKA), bf16)
    for j in range(n):
        s = score_fn(j)
        s_sc[j] = s
        a_sc[j] = jnp.max(s, axis=0, keepdims=True)
    for j in range(n):
        m_old = m_sc[j]
        m_new = jnp.maximum(m_old, a_sc[j])
        p_sc[j] = jnp.exp2(s_sc[j] - m_new).astype(bf16)
        a_sc[j] = jnp.exp2(m_old - m_new)
        m_sc[j] = m_new
    for j in range(n):
        vt = jnp.concatenate([value_fn(j), ones], axis=0)
        acc_sc[j] = a_sc[j] * acc_sc[j] + jnp.dot(vt, p_sc[j], preferred_element_type=f32)


def _init_softmax(n, m_sc, acc_sc):
    for j in range(n):
        m_sc[j] = jnp.full(m_sc.shape[1:], NEG, f32)
        acc_sc[j] = jnp.zeros(acc_sc.shape[1:], f32)


def _normalized(acc_sc, j, dv):
    return acc_sc[j, 0:dv, :] / acc_sc[j, dv:dv + 1, :]


def _softmax_scratch(n, dv):
    return [
        pltpu.VMEM((n, TKA, TQ), f32),
        pltpu.VMEM((n, TKA, TQ), bf16),
        pltpu.VMEM((n, 1, TQ), f32),
        pltpu.VMEM((n, 1, TQ), f32),
        pltpu.VMEM((n, dv + SUM_ROWS, TQ), f32),
    ]


def _dsa_kernel(qit_ref, wit_ref, qat_ref, ki_ref, ka_ref, vat_ref, o_ref,
                sc_sc, thr_sc, bias_sc, qzi_sc, qza_sc, s_sc, p_sc, a_sc, m_sc, acc_sc):
    q0 = pl.program_id(1) * TQ
    n_units = (q0 + TQ + TKA - 1) // TKA
    neg_inf = jnp.float32(-jnp.inf)

    _split_pairs_t(qit_ref, qzi_sc, IDX_HEADS // 2)
    _split_pairs_t(qat_ref, qza_sc, DSA_HEADS // 2)

    key_a = lax.broadcasted_iota(i32, (TKI, TQ), 0)
    qry_a = q0 + lax.broadcasted_iota(i32, (TKI, TQ), 1)

    def score_body(c, carry):
        k0 = pl.multiple_of(c * TKI, TKI)
        kc = ki_ref[pl.ds(k0, TKI), :]
        score = jnp.zeros((TKI, TQ), f32)
        for h in range(IDX_HEADS):
            logit = jnp.dot(kc, qzi_sc[h], preferred_element_type=f32)
            score = score + wit_ref[h:h + 1, :] * jnp.maximum(logit, 0.0)
        sc_sc[pl.ds(k0, TKI), :] = jnp.where(key_a + k0 <= qry_a, score, neg_inf)
        return carry

    lax.fori_loop(0, n_units * (TKA // TKI), score_body, 0)

    def count(pred):
        def unit_body(u, acc):
            base = pl.multiple_of(u * TKA, TKA)
            blk = sc_sc[pl.ds(base, TKA), :].reshape(TKA // SUBLANES, SUBLANES, TQ)
            idx = base + (lax.broadcasted_iota(i32, blk.shape, 0) * SUBLANES
                          + lax.broadcasted_iota(i32, blk.shape, 1))
            return acc + jnp.sum(jnp.where(pred(blk, idx), 1.0, 0.0).astype(f32), axis=0)
        acc = lax.fori_loop(0, n_units, unit_body, jnp.zeros((SUBLANES, TQ), f32))
        return jnp.broadcast_to(jnp.sum(acc, axis=0, keepdims=True), (SUBLANES, TQ))

    def key_to_float(key):
        return pltpu.bitcast(key ^ ((key >> 31) & jnp.int32(0x7FFFFFFF)), f32)

    qpos = q0 + lax.broadcasted_iota(i32, (SUBLANES, TQ), 1)
    k_sel = jnp.minimum(qpos + 1, TOPK_MAX).astype(f32)

    def bit_body(b, t):
        cand_key = t + lax.shift_left(jnp.int32(1), jnp.int32(31) - b)
        cand = key_to_float(cand_key)
        cnt = count(lambda blk, idx: blk >= cand[None])
        return jnp.where(cnt >= k_sel, cand_key, t)

    t_key = lax.fori_loop(0, 32, bit_body, jnp.full((SUBLANES, TQ), -(2 ** 31), i32))
    thr = key_to_float(t_key)
    thr_sc[...] = thr
    n_ge = count(lambda blk, idx: blk >= thr[None])

    @pl.when(jnp.max(n_ge - k_sel) > 0.0)
    def _():
        need = k_sel - count(lambda blk, idx: blk > thr[None])
        idx_bits = (sc_sc.shape[0] - 1).bit_length()

        def idx_body(b, j):
            cand = j + lax.shift_left(jnp.int32(1), jnp.int32(idx_bits - 1) - b)
            cnt = count(lambda blk, idx: (blk == thr[None]) & (idx < cand[None]))
            return jnp.where(cnt < need, cand, j)

        j_last = lax.fori_loop(0, idx_bits, idx_body, jnp.zeros((SUBLANES, TQ), i32))

        def fix_body(u, carry):
            base = pl.multiple_of(u * TKA, TKA)
            blk = sc_sc[pl.ds(base, TKA), :]
            idx = base + lax.broadcasted_iota(i32, blk.shape, 0)
            drop = (blk == thr[0:1]) & (idx > j_last[0:1])
            sc_sc[pl.ds(base, TKA), :] = jnp.where(drop, neg_inf, blk)
            return carry

        lax.fori_loop(0, n_units, fix_body, 0)

    _init_softmax(DSA_HEADS, m_sc, acc_sc)

    def attn_body(c, carry):
        k0 = pl.multiple_of(c * TKA, TKA)
        bias_sc[...] = jnp.where(sc_sc[pl.ds(k0, TKA), :] >= thr_sc[0:1, :], 0.0, NEG)

        def score_fn(h):
            pair = slice((h // 2) * LANES, (h // 2 + 1) * LANES)
            s = jnp.dot(ka_ref[pl.ds(k0, TKA), pair], qza_sc[h], preferred_element_type=f32)
            return s + bias_sc[...]

        def value_fn(h):
            return vat_ref[h * HEAD_DIM:(h + 1) * HEAD_DIM, pl.ds(k0, TKA)]

        _softmax_chunk(DSA_HEADS, score_fn, value_fn, s_sc, p_sc, a_sc, m_sc, acc_sc)
        return carry

    lax.fori_loop(0, n_units, attn_body, 0)

    ot = jnp.concatenate([_normalized(acc_sc, h, HEAD_DIM) for h in range(DSA_HEADS)], axis=0)
    o_ref[...] = ot.T.astype(o_ref.dtype)


def _dsa(qit, wit, qat, ki, ka, vat, batch, seq):
    nq = seq // TQ
    qcol = lambda b, i: (0, b * nq + i)
    resident = dict(pipeline_mode=pl.Buffered(1))
    return pl.pallas_call(
        _dsa_kernel,
        grid=(batch, nq),
        in_specs=[
            pl.BlockSpec((GROUP, TQ), qcol),
            pl.BlockSpec((WI_ROWS, TQ), qcol),
            pl.BlockSpec((GROUP, TQ), qcol),
            pl.BlockSpec((seq, LANES), lambda b, i: (b, 0), **resident),
            pl.BlockSpec((seq, GROUP), lambda b, i: (b, 0), **resident),
            pl.BlockSpec((GROUP, seq), lambda b, i: (0, b), **resident),
        ],
        out_specs=pl.BlockSpec((TQ, GROUP), lambda b, i: (b * nq + i, 0)),
        out_shape=jax.ShapeDtypeStruct((batch * seq, GROUP), bf16),
        scratch_shapes=[
            pltpu.VMEM((seq, TQ), f32),
            pltpu.VMEM((SUBLANES, TQ), f32),
            pltpu.VMEM((TKA, TQ), f32),
            pltpu.VMEM((IDX_HEADS, LANES, TQ), bf16),
            pltpu.VMEM((DSA_HEADS, LANES, TQ), bf16),
        ] + _softmax_scratch(DSA_HEADS, HEAD_DIM),
        compiler_params=pltpu.CompilerParams(
            dimension_semantics=("arbitrary", "arbitrary"), vmem_limit_bytes=56 * MIB),
        name="dsa",
    )(qit, wit, qat, ki, ka, vat)


def _diff_kernel(lq1_ref, lk1_ref, lq2_ref, lk2_ref, g_ref, qdt_ref, kd_ref, vdt_ref, o_ref,
                 qz_sc, s_sc, p_sc, a_sc, m_sc, acc_sc, *, lam_init):
    q0 = pl.program_id(1) * TQ
    n_chunks = (q0 + TQ + TKA - 1) // TKA
    n_sm = 2 * DIFF_HEADS
    _split_pairs_t(qdt_ref, qz_sc, DIFF_HEADS)
    _init_softmax(n_sm, m_sc, acc_sc)

    def chunk(c, masked):
        k0 = pl.multiple_of(c * TKA, TKA)
        if masked:
            causal = (k0 + lax.broadcasted_iota(i32, (TKA, TQ), 0)
                      <= q0 + lax.broadcasted_iota(i32, (TKA, TQ), 1))

        def score_fn(j):
            h = j // 2
            s = jnp.dot(kd_ref[pl.ds(k0, TKA), h * LANES:(h + 1) * LANES], qz_sc[j],
                        preferred_element_type=f32)
            return jnp.where(causal, s, NEG) if masked else s

        def value_fn(j):
            h = j // 2
            return vdt_ref[h * LANES:(h + 1) * LANES, pl.ds(k0, TKA)]

        _softmax_chunk(n_sm, score_fn, value_fn, s_sc, p_sc, a_sc, m_sc, acc_sc)

    def body(c, carry):
        chunk(c, False)
        return carry

    lax.fori_loop(0, n_chunks - 1, body, 0)
    chunk(n_chunks - 1, True)

    lam = (jnp.exp(jnp.sum(lq1_ref[...] * lk1_ref[...], axis=1, keepdims=True))
           - jnp.exp(jnp.sum(lq2_ref[...] * lk2_ref[...], axis=1, keepdims=True)) + lam_init)
    heads = []
    for h in range(DIFF_HEADS):
        o = (_normalized(acc_sc, 2 * h, 2 * DIFF_DIM)
             - lam * _normalized(acc_sc, 2 * h + 1, 2 * DIFF_DIM))
        heads.append(o * lax.rsqrt(jnp.mean(o * o, axis=0, keepdims=True) + EPS))
    o = jnp.concatenate(heads, axis=0).T
    g = g_ref[...]
    o_ref[...] = (o * jnp.concatenate([g] * DIFF_HEADS, axis=1) * (1.0 - lam_init)).astype(o_ref.dtype)


def _diff(lq1, lk1, lq2, lk2, g, qdt, kd, vdt, lam_init, batch, seq):
    assert TKA == 2 * TQ
    nq = seq // TQ
    const = lambda b, i: (0, 0)
    resident = dict(pipeline_mode=pl.Buffered(1))
    n_sm = 2 * DIFF_HEADS
    vec = pl.BlockSpec((1, DIFF_DIM), const)
    return pl.pallas_call(
        functools.partial(_diff_kernel, lam_init=lam_init),
        grid=(batch, nq),
        in_specs=[
            vec, vec, vec, vec,
            pl.BlockSpec((1, 2 * DIFF_DIM), const),
            pl.BlockSpec((GROUP, TQ), lambda b, i: (0, b * nq + i)),
            pl.BlockSpec((seq, GROUP), lambda b, i: (b, 0), **resident),
            pl.BlockSpec((GROUP, seq), lambda b, i: (0, b), **resident),
        ],
        out_specs=pl.BlockSpec((TQ, GROUP), lambda b, i: (b * nq + i, 0)),
        out_shape=jax.ShapeDtypeStruct((batch * seq, GROUP), bf16),
        scratch_shapes=[
            pltpu.VMEM((n_sm, LANES, TQ), bf16),
        ] + _softmax_scratch(n_sm, 2 * DIFF_DIM),
        compiler_params=pltpu.CompilerParams(
            dimension_semantics=("arbitrary", "arbitrary"), vmem_limit_bytes=48 * MIB),
        name="diff",
    )(lq1, lk1, lq2, lk2, g, qdt, kd, vdt)


def _post_kernel(x_ref, oa_ref, od_ref, wo_ref, wu_ref, wd_ref, g1_ref, g2_ref, g3_ref, o_ref):
    y = (jnp.dot(oa_ref[...], wo_ref[0:DSA_W, :], preferred_element_type=f32)
         + jnp.dot(od_ref[...], wo_ref[DSA_W:DSA_W + DIFF_W, :], preferred_element_type=f32))
    x1 = x_ref[...] + _rms(y) * g1_ref[...]
    h = (_rms(x1) * g2_ref[...]).astype(bf16)
    acc = jnp.zeros_like(x1)
    for c in range(D_FF // FF_CHUNK):
        cols = slice(c * FF_CHUNK, (c + 1) * FF_CHUNK)
        u = jnp.maximum(jnp.dot(h, wu_ref[:, cols], preferred_element_type=f32), 0.0)
        acc = acc + jnp.dot((u * u).astype(bf16), wd_ref[cols, :], preferred_element_type=f32)
    o_ref[...] = x1 + _rms(acc) * g3_ref[...]


def _post(x2d, oa, od, wo, wu, wd, g1, g2, g3):
    m = x2d.shape[0]
    tm = PROJ_TM
    row = lambda i: (i, 0)
    const = lambda i: (0, 0)
    resident = dict(pipeline_mode=pl.Buffered(1))
    gspec = pl.BlockSpec((1, D_MODEL), const)
    return pl.pallas_call(
        _post_kernel,
        grid=(m // tm,),
        in_specs=[
            pl.BlockSpec((tm, D_MODEL), row),
            pl.BlockSpec((tm, DSA_W), row),
            pl.BlockSpec((tm, DIFF_W), row),
            pl.BlockSpec((DSA_W + DIFF_W, D_MODEL), const, **resident),
            pl.BlockSpec((D_MODEL, D_FF), const, **resident),
            pl.BlockSpec((D_FF, D_MODEL), const, **resident),
            gspec, gspec, gspec,
        ],
        out_specs=pl.BlockSpec((tm, D_MODEL), row),
        out_shape=jax.ShapeDtypeStruct((m, D_MODEL), f32),
        compiler_params=pltpu.CompilerParams(
            dimension_semantics=("arbitrary",), vmem_limit_bytes=56 * MIB),
        name="post",
    )(x2d, oa, od, wo, wu, wd, g1, g2, g3)


def _rope_tables(seq):
    inv = 1.0 / (ROPE_THETA ** (jnp.arange(0, HEAD_DIM, 2, dtype=f32) / HEAD_DIM))
    ang = jnp.arange(seq, dtype=f32)[:, None] * inv[None, :]
    cos, sin = jnp.cos(ang), jnp.sin(ang)
    cs = jnp.concatenate([cos, cos, cos, cos], axis=1)
    sn = jnp.concatenate([-sin, sin, -sin, sin], axis=1)
    return cs, sn, cos.T, sin.T


def _prep_w_in(w):
    q_a, k_a, v_a, q_i, k_i, w_i, q_d, k_d, v_d = jnp.split(w, SPLIT_POINTS, axis=1)
    wn = jnp.concatenate([k_a, k_d, k_i, k_i], axis=1)
    pad = jnp.zeros((D_MODEL, WI_ROWS - IDX_HEADS), w.dtype)
    wt = jnp.concatenate([q_a * SM_SCALE, q_i, q_d * SM_SCALE, v_a, v_d, w_i, pad], axis=1).T
    return wn.astype(bf16), wt.astype(bf16)


def kernel(x, w_in, w_out, w_up, w_down, g_pre_mix, g_post_mix, g_pre_mlp, g_post_mlp,
           g_diff_sub, lambda_q1, lambda_k1, lambda_q2, lambda_k2):
    b, s, d = x.shape
    depth = w_in.shape[0]
    assert d == D_MODEL and s % TKA == 0 and s % PROJ_TM == 0 and min(TOPK_MAX, s // 4) == TOPK_MAX
    cs, sn, cst, snt = _rope_tables(s)
    x2d = x.reshape(b * s, d)
    for layer in range(depth):
        wn, wt = _prep_w_in(w_in[layer])
        ka, kd, ki, qat, qit, qdt, vat, vdt, wit = _proj(
            x2d, g_pre_mix[layer][None, :], cs, sn, cst, snt, wn, wt, s)
        oa = _dsa(qit, wit, qat, ki, ka, vat, b, s)
        lam_init = 0.8 - 0.6 * math.exp(-0.3 * layer)
        od = _diff(lambda_q1[layer][None, :], lambda_k1[layer][None, :],
                   lambda_q2[layer][None, :], lambda_k2[layer][None, :],
                   g_diff_sub[layer][None, :], qdt, kd, vdt, lam_init, b, s)
        x2d = _post(x2d, oa, od,
                    w_out[layer].astype(bf16), w_up[layer].astype(bf16), w_down[layer].astype(bf16),
                    g_post_mix[layer][None, :], g_pre_mlp[layer][None, :], g_post_mlp[layer][None, :])
    return x2d.reshape(b, s, d)
```

```python
import functools
import math

import jax
import jax.numpy as jnp
import numpy as np
from jax import lax
from jax.experimental import pallas as pl
from jax.experimental.pallas import tpu as pltpu

f32 = jnp.float32
bf16 = jnp.bfloat16
i32 = jnp.int32

D_MODEL = 1024
HEAD_DIM = 64
DSA_HEADS = 8
DIFF_HEADS = 4
DIFF_DIM = 64
IDX_HEADS = 8
IDX_DIM = 64
TOPK_MAX = 256
D_FF = 4 * D_MODEL
ROPE_THETA = 10000.0
EPS = 1e-6
NEG = -1e30

DSA_W = DSA_HEADS * HEAD_DIM
DIFF_W = DIFF_HEADS * 2 * DIFF_DIM
IN_SIZES = (DSA_W, DSA_W, DSA_W, IDX_HEADS * IDX_DIM, IDX_DIM, IDX_HEADS, DIFF_W, DIFF_W, DIFF_W)
SPLIT_POINTS = tuple(int(v) for v in np.cumsum(IN_SIZES)[:-1])
HALF = HEAD_DIM // 2

LANES = 128
SUBLANES = 8
BF16_ROWS = 16
MIB = 1024 * 1024

GROUP = 512
PROJ_TM = 512
WI_ROWS = BF16_ROWS
T_ROWS = 5 * GROUP + WI_ROWS
N_COLS = 2 * GROUP + LANES
FF_CHUNK = 1024

TQ = 256
TKI = 256
TKA = 512
SEL_PASSES = 33
CNT_ACC = 16
SUM_ROWS = BF16_ROWS
SM_SCALE = HEAD_DIM ** -0.5 * math.log2(math.e)

_NT = (((1,), (1,)), ((), ()))


def _rms(x):
    return x * lax.rsqrt(jnp.mean(x * x, axis=-1, keepdims=True) + EPS)


def _proj_kernel(x_ref, g_ref, cs_ref, sn_ref, cst_ref, snt_ref, wn_ref, wt_ref,
                 ka_ref, kd_ref, ki_ref, qat_ref, qit_ref, qdt_ref, vat_ref, vdt_ref, wit_ref):
    h = (_rms(x_ref[...]) * g_ref[...]).astype(bf16)
    tm = h.shape[0]

    cs = cs_ref[...]
    sn = sn_ref[...]

    def rope_rows(col0, width, out_ref):
        y = jnp.dot(h, wn_ref[:, col0:col0 + width], preferred_element_type=f32)
        lane = lax.broadcasted_iota(i32, (tm, width), 1)
        partner = jnp.where((lane & (HEAD_DIM - 1)) < HALF,
                            pltpu.roll(y, width - HALF, 1), pltpu.roll(y, HALF, 1))
        reps = width // LANES
        c = cs if reps == 1 else jnp.concatenate([cs] * reps, axis=1)
        s = sn if reps == 1 else jnp.concatenate([sn] * reps, axis=1)
        out_ref[...] = (y * c + partner * s).astype(out_ref.dtype)

    rope_rows(0, GROUP, ka_ref)
    rope_rows(GROUP, GROUP, kd_ref)
    rope_rows(2 * GROUP, LANES, ki_ref)

    yt = lax.dot_general(wt_ref[...], h, _NT, preferred_element_type=f32)
    ct = cst_ref[...]
    st = snt_ref[...]

    def rope_cols(row0, out_ref):
        for hd in range(GROUP // HEAD_DIM):
            r = row0 + hd * HEAD_DIM
            x1 = yt[r:r + HALF]
            x2 = yt[r + HALF:r + HEAD_DIM]
            out_ref[hd * HEAD_DIM:hd * HEAD_DIM + HALF, :] = (x1 * ct - x2 * st).astype(out_ref.dtype)
            out_ref[hd * HEAD_DIM + HALF:(hd + 1) * HEAD_DIM, :] = (x2 * ct + x1 * st).astype(out_ref.dtype)

    rope_cols(0, qat_ref)
    rope_cols(GROUP, qit_ref)
    rope_cols(2 * GROUP, qdt_ref)
    vat_ref[...] = yt[3 * GROUP:4 * GROUP].astype(bf16)
    vdt_ref[...] = yt[4 * GROUP:5 * GROUP].astype(bf16)
    idx_scale = (IDX_HEADS ** -0.5) * (IDX_DIM ** -0.5)
    wit_ref[...] = yt[5 * GROUP:5 * GROUP + WI_ROWS] * idx_scale


def _proj(x2d, g, cs, sn, cst, snt, wn, wt, seq):
    m = x2d.shape[0]
    tm = PROJ_TM
    pos_blocks = seq // tm
    row = lambda i: (i, 0)
    col = lambda i: (0, i)
    const = lambda i: (0, 0)
    rows_out = jax.ShapeDtypeStruct((m, GROUP), bf16)
    cols_out = jax.ShapeDtypeStruct((GROUP, m), bf16)
    out_shape = (rows_out, rows_out, jax.ShapeDtypeStruct((m, LANES), bf16),
                 cols_out, cols_out, cols_out, cols_out, cols_out,
                 jax.ShapeDtypeStruct((WI_ROWS, m), f32))
    rows_spec = pl.BlockSpec((tm, GROUP), row)
    cols_spec = pl.BlockSpec((GROUP, tm), col)
    return pl.pallas_call(
        _proj_kernel,
        grid=(m // tm,),
        in_specs=[
            pl.BlockSpec((tm, D_MODEL), row),
            pl.BlockSpec((1, D_MODEL), const),
            pl.BlockSpec((tm, LANES), lambda i: (i % pos_blocks, 0)),
            pl.BlockSpec((tm, LANES), lambda i: (i % pos_blocks, 0)),
            pl.BlockSpec((HALF, tm), lambda i: (0, i % pos_blocks)),
            pl.BlockSpec((HALF, tm), lambda i: (0, i % pos_blocks)),
            pl.BlockSpec((D_MODEL, N_COLS), const),
            pl.BlockSpec((T_ROWS, D_MODEL), const),
        ],
        out_specs=(rows_spec, rows_spec, pl.BlockSpec((tm, LANES), row),
                   cols_spec, cols_spec, cols_spec, cols_spec, cols_spec,
                   pl.BlockSpec((WI_ROWS, tm), col)),
        out_shape=out_shape,
        compiler_params=pltpu.CompilerParams(
            dimension_semantics=("arbitrary",), vmem_limit_bytes=56 * MIB),
        name="proj",
    )(x2d, g, cs, sn, cst, snt, wn, wt)


def _split_pairs_t(src_ref, dst_sc, n_pairs):
    zero = jnp.zeros((HEAD_DIM, dst_sc.shape[2]), dst_sc.dtype)
    for p in range(n_pairs):
        r = p * LANES
        dst_sc[2 * p, 0:HEAD_DIM, :] = src_ref[r:r + HEAD_DIM, :]
        dst_sc[2 * p, HEAD_DIM:LANES, :] = zero
        dst_sc[2 * p + 1, 0:HEAD_DIM, :] = zero
        dst_sc[2 * p + 1, HEAD_DIM:LANES, :] = src_ref[r + HEAD_DIM:r + LANES, :]


def _softmax_chunk(n, score_fn, value_fn, s_sc, p_sc, a_sc, m_sc, acc_sc):
    ones = jnp.ones((SUM_ROWS, TKA), bf16)
    for j in range(n):
        s = score_fn(j)
        s_sc[j] = s
        a_sc[j] = jnp.max(s, axis=0, keepdims=True)
    for j in range(n):
        m_old = m_sc[j]
        m_new = jnp.maximum(m_old, a_sc[j])
        p_sc[j] = jnp.exp2(s_sc[j] - m_new).astype(bf16)
        a_sc[j] = jnp.exp2(m_old - m_new)
        m_sc[j] = m_new
    for j in range(n):
        vt = jnp.concatenate([value_fn(j), ones], axis=0)
        acc_sc[j] = a_sc[j] * acc_sc[j] + jnp.dot(vt, p_sc[j], preferred_element_type=f32)


def _init_softmax(n, m_sc, acc_sc):
    for j in range(n):
        m_sc[j] = jnp.full(m_sc.shape[1:], NEG, f32)
        acc_sc[j] = jnp.zeros(acc_sc.shape[1:], f32)


def _normalized(acc_sc, j, dv):
    return acc_sc[j, 0:dv, :] / acc_sc[j, dv:dv + 1, :]


def _softmax_scratch(n, dv):
    return [
        pltpu.VMEM((n, TKA, TQ), f32),
        pltpu.VMEM((n, TKA, TQ), bf16),
        pltpu.VMEM((n, 1, TQ), f32),
        pltpu.VMEM((n, 1, TQ), f32),
        pltpu.VMEM((n, dv + SUM_ROWS, TQ), f32),
    ]


def _key_to_float(key):
    return pltpu.bitcast(key ^ ((key >> 31) & jnp.int32(0x7FFFFFFF)), f32)


_CNT_SHAPE = (TKA // (CNT_ACC * SUBLANES), CNT_ACC, SUBLANES, TQ)


def _unit_count(sc_sc, u, pred):
    base = pl.multiple_of(u * TKA, TKA)
    blk = sc_sc[pl.ds(base, TKA), :].reshape(_CNT_SHAPE)
    idx = base + ((lax.broadcasted_iota(i32, _CNT_SHAPE, 0) * CNT_ACC
                   + lax.broadcasted_iota(i32, _CNT_SHAPE, 1)) * SUBLANES
                  + lax.broadcasted_iota(i32, _CNT_SHAPE, 2))
    part = jnp.sum(jnp.where(pred(blk, idx), 1.0, 0.0).astype(f32), axis=0)
    return jnp.sum(part, axis=0)


def _count_total(acc):
    return jnp.broadcast_to(jnp.sum(acc, axis=0, keepdims=True), (SUBLANES, TQ))


def _seldiff_kernel(lq1_ref, lk1_ref, lq2_ref, lk2_ref, g_ref, qit_ref, wit_ref, ki_ref,
                    qdt_ref, kd_ref, vdt_ref, od_ref, bias_ref,
                    sc_sc, qzi_sc, qz_sc, s_sc, p_sc, a_sc, m_sc, acc_sc, *, lam_init):
    q0 = pl.program_id(1) * TQ
    n_units = (q0 + TQ + TKA - 1) // TKA
    n_sm = 2 * DIFF_HEADS
    neg_inf = jnp.float32(-jnp.inf)

    _split_pairs_t(qit_ref, qzi_sc, IDX_HEADS // 2)
    _split_pairs_t(qdt_ref, qz_sc, DIFF_HEADS)
    _init_softmax(n_sm, m_sc, acc_sc)

    key_a = lax.broadcasted_iota(i32, (TKI, TQ), 0)
    qry_a = q0 + lax.broadcasted_iota(i32, (TKI, TQ), 1)

    def score_body(c, carry):
        k0 = pl.multiple_of(c * TKI, TKI)
        kc = ki_ref[pl.ds(k0, TKI), :]
        score = jnp.zeros((TKI, TQ), f32)
        for h in range(IDX_HEADS):
            logit = jnp.dot(kc, qzi_sc[h], preferred_element_type=f32)
            score = score + wit_ref[h:h + 1, :] * jnp.maximum(logit, 0.0)
        sc_sc[pl.ds(k0, TKI), :] = jnp.where(key_a + k0 <= qry_a, score, neg_inf)
        return carry

    lax.fori_loop(0, n_units * (TKA // TKI), score_body, 0)

    qpos = q0 + lax.broadcasted_iota(i32, (SUBLANES, TQ), 1)
    k_sel = jnp.minimum(qpos + 1, TOPK_MAX).astype(f32)

    def select_steps(state):
        for _ in range(SEL_PASSES):
            p, u, t_key, acc, n_ge = state
            bit = lax.shift_left(jnp.int32(1), jnp.maximum(jnp.int32(31) - p, 0))
            cand_key = t_key + jnp.where(p < 32, bit, 0)
            cand = _key_to_float(cand_key)
            acc = acc + _unit_count(sc_sc, u, lambda blk, idx: blk >= cand[None])
            cnt = _count_total(acc)
            last = u == n_units - 1
            t_key = jnp.where(jnp.logical_and(last, p < 32),
                              jnp.where(cnt >= k_sel, cand_key, t_key), t_key)
            n_ge = jnp.where(jnp.logical_and(last, p == 32), cnt, n_ge)
            acc = jnp.where(last, jnp.zeros_like(acc), acc)
            state = (jnp.where(last, p + 1, p), jnp.where(last, 0, u + 1), t_key, acc, n_ge)
        return state

    def chunk(c, masked):
        k0 = pl.multiple_of(c * TKA, TKA)
        if masked:
            causal = (k0 + lax.broadcasted_iota(i32, (TKA, TQ), 0)
                      <= q0 + lax.broadcasted_iota(i32, (TKA, TQ), 1))

        def score_fn(j):
            h = j // 2
            s = jnp.dot(kd_ref[pl.ds(k0, TKA), h * LANES:(h + 1) * LANES], qz_sc[j],
                        preferred_element_type=f32)
            return jnp.where(causal, s, NEG) if masked else s

        def value_fn(j):
            h = j // 2
            return vdt_ref[h * LANES:(h + 1) * LANES, pl.ds(k0, TKA)]

        _softmax_chunk(n_sm, score_fn, value_fn, s_sc, p_sc, a_sc, m_sc, acc_sc)

    def body(c, state):
        chunk(c, False)
        return select_steps(state)

    state = (jnp.int32(0), jnp.int32(0), jnp.full((SUBLANES, TQ), -(2 ** 31), i32),
             jnp.zeros((SUBLANES, TQ), f32), jnp.zeros((SUBLANES, TQ), f32))
    state = lax.fori_loop(0, n_units - 1, body, state)
    chunk(n_units - 1, True)
    _, _, t_key, _, n_ge = select_steps(state)
    thr = _key_to_float(t_key)

    def count(pred):
        acc = lax.fori_loop(0, n_units, lambda u, acc: acc + _unit_count(sc_sc, u, pred),
                            jnp.zeros((SUBLANES, TQ), f32))
        return _count_total(acc)

    @pl.when(jnp.max(n_ge - k_sel) > 0.0)
    def _():
        need = k_sel - count(lambda blk, idx: blk > thr[None])
        idx_bits = (sc_sc.shape[0] - 1).bit_length()

        def idx_body(b, j):
            cand = j + lax.shift_left(jnp.int32(1), jnp.int32(idx_bits - 1) - b)
            cnt = count(lambda blk, idx: (blk == thr[None]) & (idx < cand[None]))
            return jnp.where(cnt < need, cand, j)

        j_last = lax.fori_loop(0, idx_bits, idx_body, jnp.zeros((SUBLANES, TQ), i32))

        def fix_body(u, carry):
            base = pl.multiple_of(u * TKA, TKA)
            blk = sc_sc[pl.ds(base, TKA), :]
            idx = base + lax.broadcasted_iota(i32, blk.shape, 0)
            drop = (blk == thr[0:1]) & (idx > j_last[0:1])
            sc_sc[pl.ds(base, TKA), :] = jnp.where(drop, neg_inf, blk)
            return carry

        lax.fori_loop(0, n_units, fix_body, 0)

    def bias_body(u, carry):
        base = pl.multiple_of(u * TKA, TKA)
        keep = sc_sc[pl.ds(base, TKA), :] >= thr[0:1]
        bias_ref[pl.ds(base, TKA), :] = jnp.where(keep, 0.0, NEG).astype(bias_ref.dtype)
        return carry

    def fill_body(u, carry):
        base = pl.multiple_of(u * TKA, TKA)
        bias_ref[pl.ds(base, TKA), :] = jnp.full((TKA, TQ), NEG, bias_ref.dtype)
        return carry

    lax.fori_loop(0, n_units, bias_body, 0)
    lax.fori_loop(n_units, sc_sc.shape[0] // TKA, fill_body, 0)

    lam = (jnp.exp(jnp.sum(lq1_ref[...] * lk1_ref[...], axis=1, keepdims=True))
           - jnp.exp(jnp.sum(lq2_ref[...] * lk2_ref[...], axis=1, keepdims=True)) + lam_init)
    heads = []
    for h in range(DIFF_HEADS):
        o = (_normalized(acc_sc, 2 * h, 2 * DIFF_DIM)
             - lam * _normalized(acc_sc, 2 * h + 1, 2 * DIFF_DIM))
        heads.append(o * lax.rsqrt(jnp.mean(o * o, axis=0, keepdims=True) + EPS))
    o = jnp.concatenate(heads, axis=0).T
    g = g_ref[...]
    od_ref[...] = (o * jnp.concatenate([g] * DIFF_HEADS, axis=1) * (1.0 - lam_init)).astype(od_ref.dtype)


def _seldiff(lq1, lk1, lq2, lk2, g, qit, wit, ki, qdt, kd, vdt, lam_init, batch, seq):
    assert TKA == 2 * TQ
    nq = seq // TQ
    const = lambda b, i: (0, 0)
    qcol = lambda b, i: (0, b * nq + i)
    resident = dict(pipeline_mode=pl.Buffered(1))
    n_sm = 2 * DIFF_HEADS
    vec = pl.BlockSpec((1, DIFF_DIM), const)
    return pl.pallas_call(
        functools.partial(_seldiff_kernel, lam_init=lam_init),
        grid=(batch, nq),
        in_specs=[
            vec, vec, vec, vec,
            pl.BlockSpec((1, 2 * DIFF_DIM), const),
            pl.BlockSpec((GROUP, TQ), qcol),
            pl.BlockSpec((WI_ROWS, TQ), qcol),
            pl.BlockSpec((seq, LANES), lambda b, i: (b, 0), **resident),
            pl.BlockSpec((GROUP, TQ), qcol),
            pl.BlockSpec((seq, GROUP), lambda b, i: (b, 0), **resident),
            pl.BlockSpec((GROUP, seq), lambda b, i: (0, b), **resident),
        ],
        out_specs=(pl.BlockSpec((TQ, GROUP), lambda b, i: (b * nq + i, 0)),
                   pl.BlockSpec((seq, TQ), qcol)),
        out_shape=(jax.ShapeDtypeStruct((batch * seq, GROUP), bf16),
                   jax.ShapeDtypeStruct((seq, batch * seq), bf16)),
        scratch_shapes=[
            pltpu.VMEM((seq, TQ), f32),
            pltpu.VMEM((IDX_HEADS, LANES, TQ), bf16),
            pltpu.VMEM((n_sm, LANES, TQ), bf16),
        ] + _softmax_scratch(n_sm, 2 * DIFF_DIM),
        compiler_params=pltpu.CompilerParams(
            dimension_semantics=("arbitrary", "arbitrary"), vmem_limit_bytes=56 * MIB),
        name="seldiff",
    )(lq1, lk1, lq2, lk2, g, qit, wit, ki, qdt, kd, vdt)


def _dsa_kernel(qat_ref, bias_ref, ka_ref, vat_ref, o_ref,
                bias_sc, qza_sc, s_sc, p_sc, a_sc, m_sc, acc_sc):
    q0 = pl.program_id(1) * TQ
    n_units = (q0 + TQ + TKA - 1) // TKA
    _split_pairs_t(qat_ref, qza_sc, DSA_HEADS // 2)
    _init_softmax(DSA_HEADS, m_sc, acc_sc)

    def attn_body(c, carry):
        k0 = pl.multiple_of(c * TKA, TKA)
        bias_sc[...] = bias_ref[pl.ds(k0, TKA), :].astype(f32)

        def score_fn(h):
            pair = slice((h // 2) * LANES, (h // 2 + 1) * LANES)
            s = jnp.dot(ka_ref[pl.ds(k0, TKA), pair], qza_sc[h], preferred_element_type=f32)
            return s + bias_sc[...]

        def value_fn(h):
            return vat_ref[h * HEAD_DIM:(h + 1) * HEAD_DIM, pl.ds(k0, TKA)]

        _softmax_chunk(DSA_HEADS, score_fn, value_fn, s_sc, p_sc, a_sc, m_sc, acc_sc)
        return carry

    lax.fori_loop(0, n_units, attn_body, 0)

    ot = jnp.concatenate([_normalized(acc_sc, h, HEAD_DIM) for h in range(DSA_HEADS)], axis=0)
    o_ref[...] = ot.T.astype(o_ref.dtype)


def _dsa(qat, bias, ka, vat, batch, seq):
    nq = seq // TQ
    qcol = lambda b, i: (0, b * nq + i)
    resident = dict(pipeline_mode=pl.Buffered(1))
    return pl.pallas_call(
        _dsa_kernel,
        grid=(batch, nq),
        in_specs=[
            pl.BlockSpec((GROUP, TQ), qcol),
            pl.BlockSpec((seq, TQ), qcol),
            pl.BlockSpec((seq, GROUP), lambda b, i: (b, 0), **resident),
            pl.BlockSpec((GROUP, seq), lambda b, i: (0, b), **resident),
        ],
        out_specs=pl.BlockSpec((TQ, GROUP), lambda b, i: (b * nq + i, 0)),
        out_shape=jax.ShapeDtypeStruct((batch * seq, GROUP), bf16),
        scratch_shapes=[
            pltpu.VMEM((TKA, TQ), f32),
            pltpu.VMEM((DSA_HEADS, LANES, TQ), bf16),
        ] + _softmax_scratch(DSA_HEADS, HEAD_DIM),
        compiler_params=pltpu.CompilerParams(
            dimension_semantics=("arbitrary", "arbitrary"), vmem_limit_bytes=48 * MIB),
        name="dsa",
    )(qat, bias, ka, vat)


def _post_kernel(x_ref, oa_ref, od_ref, wo_ref, wu_ref, wd_ref, g1_ref, g2_ref, g3_ref, o_ref):
    y = (jnp.dot(oa_ref[...], wo_ref[0:DSA_W, :], preferred_element_type=f32)
         + jnp.dot(od_ref[...], wo_ref[DSA_W:DSA_W + DIFF_W, :], preferred_element_type=f32))
    x1 = x_ref[...] + _rms(y) * g1_ref[...]
    h = (_rms(x1) * g2_ref[...]).astype(bf16)
    acc = jnp.zeros_like(x1)
    for c in range(D_FF // FF_CHUNK):
        cols = slice(c * FF_CHUNK, (c + 1) * FF_CHUNK)
        u = jnp.maximum(jnp.dot(h, wu_ref[:, cols], preferred_element_type=f32), 0.0)
        acc = acc + jnp.dot((u * u).astype(bf16), wd_ref[cols, :], preferred_element_type=f32)
    o_ref[...] = x1 + _rms(acc) * g3_ref[...]


def _post(x2d, oa, od, wo, wu, wd, g1, g2, g3):
    m = x2d.shape[0]
    tm = PROJ_TM
    row = lambda i: (i, 0)
    const = lambda i: (0, 0)
    resident = dict(pipeline_mode=pl.Buffered(1))
    gspec = pl.BlockSpec((1, D_MODEL), const)
    return pl.pallas_call(
        _post_kernel,
        grid=(m // tm,),
        in_specs=[
            pl.BlockSpec((tm, D_MODEL), row),
            pl.BlockSpec((tm, DSA_W), row),
            pl.BlockSpec((tm, DIFF_W), row),
            pl.BlockSpec((DSA_W + DIFF_W, D_MODEL), const, **resident),
            pl.BlockSpec((D_MODEL, D_FF), const, **resident),
            pl.BlockSpec((D_FF, D_MODEL), const, **resident),
            gspec, gspec, gspec,
        ],
        out_specs=pl.BlockSpec((tm, D_MODEL), row),
        out_shape=jax.ShapeDtypeStruct((m, D_MODEL), f32),
        compiler_params=pltpu.CompilerParams(
            dimension_semantics=("arbitrary",), vmem_limit_bytes=56 * MIB),
        name="post",
    )(x2d, oa, od, wo, wu, wd, g1, g2, g3)


def _rope_tables(seq):
    inv = 1.0 / (ROPE_THETA ** (jnp.arange(0, HEAD_DIM, 2, dtype=f32) / HEAD_DIM))
    ang = jnp.arange(seq, dtype=f32)[:, None] * inv[None, :]
    cos, sin = jnp.cos(ang), jnp.sin(ang)
    cs = jnp.concatenate([cos, cos, cos, cos], axis=1)
    sn = jnp.concatenate([-sin, sin, -sin, sin], axis=1)
    return cs, sn, cos.T, sin.T


def _prep_w_in(w):
    q_a, k_a, v_a, q_i, k_i, w_i, q_d, k_d, v_d = jnp.split(w, SPLIT_POINTS, axis=1)
    wn = jnp.concatenate([k_a, k_d, k_i, k_i], axis=1)
    pad = jnp.zeros((D_MODEL, WI_ROWS - IDX_HEADS), w.dtype)
    wt = jnp.concatenate([q_a * SM_SCALE, q_i, q_d * SM_SCALE, v_a, v_d, w_i, pad], axis=1).T
    return wn.astype(bf16), wt.astype(bf16)


def kernel(x, w_in, w_out, w_up, w_down, g_pre_mix, g_post_mix, g_pre_mlp, g_post_mlp,
           g_diff_sub, lambda_q1, lambda_k1, lambda_q2, lambda_k2):
    b, s, d = x.shape
    depth = w_in.shape[0]
    assert d == D_MODEL and s % TKA == 0 and s % PROJ_TM == 0 and min(TOPK_MAX, s // 4) == TOPK_MAX
    cs, sn, cst, snt = _rope_tables(s)
    x2d = x.reshape(b * s, d)
    for layer in range(depth):
        wn, wt = _prep_w_in(w_in[layer])
        ka, kd, ki, qat, qit, qdt, vat, vdt, wit = _proj(
            x2d, g_pre_mix[layer][None, :], cs, sn, cst, snt, wn, wt, s)
        lam_init = 0.8 - 0.6 * math.exp(-0.3 * layer)
        od, bias = _seldiff(lambda_q1[layer][None, :], lambda_k1[layer][None, :],
                            lambda_q2[layer][None, :], lambda_k2[layer][None, :],
                            g_diff_sub[layer][None, :], qit, wit, ki, qdt, kd, vdt, lam_init, b, s)
        oa = _dsa(qat, bias, ka, vat, b, s)
        x2d = _post(x2d, oa, od,
                    w_out[layer].astype(bf16), w_up[layer].astype(bf16), w_down[layer].astype(bf16),
                    g_post_mix[layer][None, :], g_pre_mlp[layer][None, :], g_post_mlp[layer][None, :])
    return x2d.reshape(b, s, d)
```

```python
import functools
import math

import jax
import jax.numpy as jnp
import numpy as np
from jax import lax
from jax.experimental import pallas as pl
from jax.experimental.pallas import tpu as pltpu

f32 = jnp.float32
bf16 = jnp.bfloat16
i32 = jnp.int32

D_MODEL = 1024
HEAD_DIM = 64
DSA_HEADS = 8
DIFF_HEADS = 4
DIFF_DIM = 64
IDX_HEADS = 8
IDX_DIM = 64
TOPK_MAX = 256
D_FF = 4 * D_MODEL
ROPE_THETA = 10000.0
EPS = 1e-6
NEG = -1e30

DSA_W = DSA_HEADS * HEAD_DIM
DIFF_W = DIFF_HEADS * 2 * DIFF_DIM
IN_SIZES = (DSA_W, DSA_W, DSA_W, IDX_HEADS * IDX_DIM, IDX_DIM, IDX_HEADS, DIFF_W, DIFF_W, DIFF_W)
SPLIT_POINTS = tuple(int(v) for v in np.cumsum(IN_SIZES)[:-1])
HALF = HEAD_DIM // 2

LANES = 128
SUBLANES = 8
BF16_ROWS = 16
MIB = 1024 * 1024

GROUP = 512
PROJ_TM = 512
WI_ROWS = BF16_ROWS
T_ROWS = 5 * GROUP + WI_ROWS
N_COLS = 2 * GROUP + LANES
FF_CHUNK = 1024

TQ = 256
TKI = 256
TKA = 512
SEL_PASSES = 32
SEL_FILL = (2, 1, 1)
CNT_ROWS = 32
CNT_ACC = 8
DSA_S_ROWS = 256
SUM_ROWS = BF16_ROWS
SM_SCALE = HEAD_DIM ** -0.5 * math.log2(math.e)

_NT = (((1,), (1,)), ((), ()))


def _rms(x):
    return x * lax.rsqrt(jnp.mean(x * x, axis=-1, keepdims=True) + EPS)


def _proj_kernel(x_ref, g_ref, cs_ref, sn_ref, cst_ref, snt_ref, wn_ref, wt_ref,
                 ka_ref, kd_ref, ki_ref, qat_ref, qit_ref, qdt_ref, vat_ref, vdt_ref, wit_ref):
    h = (_rms(x_ref[...]) * g_ref[...]).astype(bf16)
    tm = h.shape[0]

    cs = cs_ref[...]
    sn = sn_ref[...]

    def rope_rows(col0, width, out_ref):
        y = jnp.dot(h, wn_ref[:, col0:col0 + width], preferred_element_type=f32)
        lane = lax.broadcasted_iota(i32, (tm, width), 1)
        partner = jnp.where((lane & (HEAD_DIM - 1)) < HALF,
                            pltpu.roll(y, width - HALF, 1), pltpu.roll(y, HALF, 1))
        reps = width // LANES
        c = cs if reps == 1 else jnp.concatenate([cs] * reps, axis=1)
        s = sn if reps == 1 else jnp.concatenate([sn] * reps, axis=1)
        out_ref[...] = (y * c + partner * s).astype(out_ref.dtype)

    rope_rows(0, GROUP, ka_ref)
    rope_rows(GROUP, GROUP, kd_ref)
    rope_rows(2 * GROUP, LANES, ki_ref)

    yt = lax.dot_general(wt_ref[...], h, _NT, preferred_element_type=f32)
    ct = cst_ref[...]
    st = snt_ref[...]

    def rope_cols(row0, out_ref):
        for hd in range(GROUP // HEAD_DIM):
            r = row0 + hd * HEAD_DIM
            x1 = yt[r:r + HALF]
            x2 = yt[r + HALF:r + HEAD_DIM]
            out_ref[hd * HEAD_DIM:hd * HEAD_DIM + HALF, :] = (x1 * ct - x2 * st).astype(out_ref.dtype)
            out_ref[hd * HEAD_DIM + HALF:(hd + 1) * HEAD_DIM, :] = (x2 * ct + x1 * st).astype(out_ref.dtype)

    rope_cols(0, qat_ref)
    rope_cols(GROUP, qit_ref)
    rope_cols(2 * GROUP, qdt_ref)
    vat_ref[...] = yt[3 * GROUP:4 * GROUP].astype(bf16)
    vdt_ref[...] = yt[4 * GROUP:5 * GROUP].astype(bf16)
    idx_scale = (IDX_HEADS ** -0.5) * (IDX_DIM ** -0.5)
    wit_ref[...] = yt[5 * GROUP:5 * GROUP + WI_ROWS] * idx_scale


def _proj(x2d, g, cs, sn, cst, snt, wn, wt, seq):
    m = x2d.shape[0]
    tm = PROJ_TM
    pos_blocks = seq // tm
    row = lambda i: (i, 0)
    col = lambda i: (0, i)
    const = lambda i: (0, 0)
    rows_out = jax.ShapeDtypeStruct((m, GROUP), bf16)
    cols_out = jax.ShapeDtypeStruct((GROUP, m), bf16)
    out_shape = (rows_out, rows_out, jax.ShapeDtypeStruct((m, LANES), bf16),
                 cols_out, cols_out, cols_out, cols_out, cols_out,
                 jax.ShapeDtypeStruct((WI_ROWS, m), f32))
    rows_spec = pl.BlockSpec((tm, GROUP), row)
    cols_spec = pl.BlockSpec((GROUP, tm), col)
    return pl.pallas_call(
        _proj_kernel,
        grid=(m // tm,),
        in_specs=[
            pl.BlockSpec((tm, D_MODEL), row),
            pl.BlockSpec((1, D_MODEL), const),
            pl.BlockSpec((tm, LANES), lambda i: (i % pos_blocks, 0)),
            pl.BlockSpec((tm, LANES), lambda i: (i % pos_blocks, 0)),
            pl.BlockSpec((HALF, tm), lambda i: (0, i % pos_blocks)),
            pl.BlockSpec((HALF, tm), lambda i: (0, i % pos_blocks)),
            pl.BlockSpec((D_MODEL, N_COLS), const),
            pl.BlockSpec((T_ROWS, D_MODEL), const),
        ],
        out_specs=(rows_spec, rows_spec, pl.BlockSpec((tm, LANES), row),
                   cols_spec, cols_spec, cols_spec, cols_spec, cols_spec,
                   pl.BlockSpec((WI_ROWS, tm), col)),
        out_shape=out_shape,
        compiler_params=pltpu.CompilerParams(
            dimension_semantics=("arbitrary",), vmem_limit_bytes=56 * MIB),
        name="proj",
    )(x2d, g, cs, sn, cst, snt, wn, wt)


def _split_pairs_t(src_ref, dst_sc, n_pairs):
    zero = jnp.zeros((HEAD_DIM, dst_sc.shape[2]), dst_sc.dtype)
    for p in range(n_pairs):
        r = p * LANES
        dst_sc[2 * p, 0:HEAD_DIM, :] = src_ref[r:r + HEAD_DIM, :]
        dst_sc[2 * p, HEAD_DIM:LANES, :] = zero
        dst_sc[2 * p + 1, 0:HEAD_DIM, :] = zero
        dst_sc[2 * p + 1, HEAD_DIM:LANES, :] = src_ref[r + HEAD_DIM:r + LANES, :]


def _softmax_chunk(n, score_fn, value_fn, s_sc, p_sc, a_sc, m_sc, acc_sc, filler=None, s_rows=TKA):
    ones = jnp.ones((SUM_ROWS, TKA), bf16)
    fill = filler if filler is not None else (lambda phase: None)
    for j in range(n):
        mx = None
        for r in range(0, TKA, s_rows):
            s = score_fn(j, r, s_rows)
            s_sc[j, r:r + s_rows, :] = s
            bm = jnp.max(s, axis=0, keepdims=True)
            mx = bm if mx is None else jnp.maximum(mx, bm)
        a_sc[j] = mx
        fill(0)
    for j in range(n):
        m_old = m_sc[j]
        m_new = jnp.maximum(m_old, a_sc[j])
        p_sc[j] = jnp.exp2(s_sc[j] - m_new).astype(bf16)
        a_sc[j] = jnp.exp2(m_old - m_new)
        m_sc[j] = m_new
        fill(1)
    for j in range(n):
        vt = jnp.concatenate([value_fn(j), ones], axis=0)
        acc_sc[j] = a_sc[j] * acc_sc[j] + jnp.dot(vt, p_sc[j], preferred_element_type=f32)
        fill(2)


def _init_softmax(n, m_sc, acc_sc):
    for j in range(n):
        m_sc[j] = jnp.full(m_sc.shape[1:], NEG, f32)
        acc_sc[j] = jnp.zeros(acc_sc.shape[1:], f32)


def _normalized(acc_sc, j, dv):
    return acc_sc[j, 0:dv, :] / acc_sc[j, dv:dv + 1, :]


def _softmax_scratch(n, dv):
    return [
        pltpu.VMEM((n, TKA, TQ), f32),
        pltpu.VMEM((n, TKA, TQ), bf16),
        pltpu.VMEM((n, 1, TQ), f32),
        pltpu.VMEM((n, 1, TQ), f32),
        pltpu.VMEM((n, dv + SUM_ROWS, TQ), f32),
    ]


def _key_to_float(key):
    return pltpu.bitcast(key ^ ((key >> 31) & jnp.int32(0x7FFFFFFF)), f32)


def _unit_count(sc_sc, u, pred, lanes=slice(0, TQ)):
    base = pl.multiple_of(u * TKA, TKA)
    width = lanes.stop - lanes.start
    shape = (CNT_ROWS // SUBLANES, SUBLANES, width)
    sub = (lax.broadcasted_iota(i32, shape, 0) * SUBLANES + lax.broadcasted_iota(i32, shape, 1))
    parts = [jnp.zeros((SUBLANES, width), f32) for _ in range(CNT_ACC)]
    for j in range(TKA // CNT_ROWS):
        blk = sc_sc[pl.ds(base + j * CNT_ROWS, CNT_ROWS), lanes].reshape(shape)
        ind = jnp.where(pred(blk, base + j * CNT_ROWS + sub), 1.0, 0.0).astype(f32)
        parts[j % CNT_ACC] = parts[j % CNT_ACC] + jnp.sum(ind, axis=0)
    return functools.reduce(lambda a, b: a + b, parts)


def _count_total(acc):
    return jnp.broadcast_to(jnp.sum(acc, axis=0, keepdims=True), acc.shape)


def _seldiff_kernel(lq1_ref, lk1_ref, lq2_ref, lk2_ref, g_ref, qit_ref, wit_ref, ki_ref,
                    qdt_ref, kd_ref, vdt_ref, od_ref, bias_ref,
                    sc_sc, qzi_sc, qz_sc, s_sc, p_sc, a_sc, m_sc, acc_sc, *, lam_init):
    q0 = pl.program_id(1) * TQ
    n_units = (q0 + TQ + TKA - 1) // TKA
    n_sm = 2 * DIFF_HEADS
    neg_inf = jnp.float32(-jnp.inf)

    _split_pairs_t(qit_ref, qzi_sc, IDX_HEADS // 2)
    _split_pairs_t(qdt_ref, qz_sc, DIFF_HEADS)
    _init_softmax(n_sm, m_sc, acc_sc)

    key_a = lax.broadcasted_iota(i32, (TKI, TQ), 0)
    qry_a = q0 + lax.broadcasted_iota(i32, (TKI, TQ), 1)

    def score_body(c, carry):
        k0 = pl.multiple_of(c * TKI, TKI)
        kc = ki_ref[pl.ds(k0, TKI), :]
        score = jnp.zeros((TKI, TQ), f32)
        for h in range(IDX_HEADS):
            logit = jnp.dot(kc, qzi_sc[h], preferred_element_type=f32)
            score = score + wit_ref[h:h + 1, :] * jnp.maximum(logit, 0.0)
        sc_sc[pl.ds(k0, TKI), :] = jnp.where(key_a + k0 <= qry_a, score, neg_inf)
        return carry

    lax.fori_loop(0, n_units * (TKA // TKI), score_body, 0)

    qpos = q0 + lax.broadcasted_iota(i32, (SUBLANES, TQ), 1)
    k_sel = jnp.minimum(qpos + 1, TOPK_MAX).astype(f32)

    lane_blocks = [slice(b * LANES, (b + 1) * LANES) for b in range(TQ // LANES)]

    def select_step(state):
        p, u, blocks = state
        bit = lax.shift_left(jnp.int32(1), jnp.int32(31) - p)
        last = u == n_units - 1
        new_blocks = []
        for lanes, (t_key, acc, n_ge) in zip(lane_blocks, blocks):
            cand_key = t_key + bit
            cand = _key_to_float(cand_key)
            acc = acc + _unit_count(sc_sc, u, lambda blk, idx: blk >= cand[None], lanes)
            cnt = _count_total(acc)
            take = jnp.logical_and(last, cnt >= k_sel[:, lanes])
            new_blocks.append((jnp.where(take, cand_key, t_key),
                               jnp.where(last, jnp.zeros_like(acc), acc),
                               jnp.where(take, cnt, n_ge)))
        return jnp.where(last, p + 1, p), jnp.where(last, 0, u + 1), tuple(new_blocks)

    def chunk(c, masked, state):
        k0 = pl.multiple_of(c * TKA, TKA)
        box = [state]

        def filler(phase):
            for _ in range(SEL_FILL[phase]):
                box[0] = select_step(box[0])

        if masked:
            causal = (k0 + lax.broadcasted_iota(i32, (TKA, TQ), 0)
                      <= q0 + lax.broadcasted_iota(i32, (TKA, TQ), 1))

        def score_fn(j, r, rows):
            h = j // 2
            s = jnp.dot(kd_ref[pl.ds(k0 + r, rows), h * LANES:(h + 1) * LANES], qz_sc[j],
                        preferred_element_type=f32)
            return jnp.where(causal[r:r + rows], s, NEG) if masked else s

        def value_fn(j):
            h = j // 2
            return vdt_ref[h * LANES:(h + 1) * LANES, pl.ds(k0, TKA)]

        _softmax_chunk(n_sm, score_fn, value_fn, s_sc, p_sc, a_sc, m_sc, acc_sc, filler)
        return box[0]

    assert n_sm * sum(SEL_FILL) == SEL_PASSES
    n_causal = (qpos + 1).astype(f32)
    state = (jnp.int32(0), jnp.int32(0),
             tuple((jnp.full((SUBLANES, LANES), -(2 ** 31), i32), jnp.zeros((SUBLANES, LANES), f32),
                    n_causal[:, lanes]) for lanes in lane_blocks))
    state = lax.fori_loop(0, n_units - 1, lambda c, st: chunk(c, False, st), state)
    _, _, blocks = chunk(n_units - 1, True, state)
    thr = _key_to_float(jnp.concatenate([blk[0] for blk in blocks], axis=1))
    n_ge = jnp.concatenate([blk[2] for blk in blocks], axis=1)

    def count(pred):
        acc = lax.fori_loop(0, n_units, lambda u, acc: acc + _unit_count(sc_sc, u, pred),
                            jnp.zeros((SUBLANES, TQ), f32))
        return _count_total(acc)

    @pl.when(jnp.max(n_ge - k_sel) > 0.0)
    def _():
        need = k_sel - count(lambda blk, idx: blk > thr[None])
        idx_bits = (sc_sc.shape[0] - 1).bit_length()

        def idx_body(b, j):
            cand = j + lax.shift_left(jnp.int32(1), jnp.int32(idx_bits - 1) - b)
            cnt = count(lambda blk, idx: (blk == thr[None]) & (idx < cand[None]))
            return jnp.where(cnt < need, cand, j)

        j_last = lax.fori_loop(0, idx_bits, idx_body, jnp.zeros((SUBLANES, TQ), i32))

        def fix_body(u, carry):
            base = pl.multiple_of(u * TKA, TKA)
            blk = sc_sc[pl.ds(base, TKA), :]
            idx = base + lax.broadcasted_iota(i32, blk.shape, 0)
            drop = (blk == thr[0:1]) & (idx > j_last[0:1])
            sc_sc[pl.ds(base, TKA), :] = jnp.where(drop, neg_inf, blk)
            return carry

        lax.fori_loop(0, n_units, fix_body, 0)

    def bias_body(u, carry):
        base = pl.multiple_of(u * TKA, TKA)
        keep = sc_sc[pl.ds(base, TKA), :] >= thr[0:1]
        bias_ref[pl.ds(base, TKA), :] = jnp.where(keep, 0.0, NEG).astype(bias_ref.dtype)
        return carry

    def fill_body(u, carry):
        base = pl.multiple_of(u * TKA, TKA)
        bias_ref[pl.ds(base, TKA), :] = jnp.full((TKA, TQ), NEG, bias_ref.dtype)
        return carry

    lax.fori_loop(0, n_units, bias_body, 0)
    lax.fori_loop(n_units, sc_sc.shape[0] // TKA, fill_body, 0)

    lam = (jnp.exp(jnp.sum(lq1_ref[...] * lk1_ref[...], axis=1, keepdims=True))
           - jnp.exp(jnp.sum(lq2_ref[...] * lk2_ref[...], axis=1, keepdims=True)) + lam_init)
    heads = []
    for h in range(DIFF_HEADS):
        o = (_normalized(acc_sc, 2 * h, 2 * DIFF_DIM)
             - lam * _normalized(acc_sc, 2 * h + 1, 2 * DIFF_DIM))
        heads.append(o * lax.rsqrt(jnp.mean(o * o, axis=0, keepdims=True) + EPS))
    o = jnp.concatenate(heads, axis=0).T
    g = g_ref[...]
    od_ref[...] = (o * jnp.concatenate([g] * DIFF_HEADS, axis=1) * (1.0 - lam_init)).astype(od_ref.dtype)


def _seldiff(lq1, lk1, lq2, lk2, g, qit, wit, ki, qdt, kd, vdt, lam_init, batch, seq):
    assert TKA == 2 * TQ
    nq = seq // TQ
    const = lambda b, i: (0, 0)
    qcol = lambda b, i: (0, b * nq + i)
    resident = dict(pipeline_mode=pl.Buffered(1))
    n_sm = 2 * DIFF_HEADS
    vec = pl.BlockSpec((1, DIFF_DIM), const)
    return pl.pallas_call(
        functools.partial(_seldiff_kernel, lam_init=lam_init),
        grid=(batch, nq),
        in_specs=[
            vec, vec, vec, vec,
            pl.BlockSpec((1, 2 * DIFF_DIM), const),
            pl.BlockSpec((GROUP, TQ), qcol),
            pl.BlockSpec((WI_ROWS, TQ), qcol),
            pl.BlockSpec((seq, LANES), lambda b, i: (b, 0), **resident),
            pl.BlockSpec((GROUP, TQ), qcol),
            pl.BlockSpec((seq, GROUP), lambda b, i: (b, 0), **resident),
            pl.BlockSpec((GROUP, seq), lambda b, i: (0, b), **resident),
        ],
        out_specs=(pl.BlockSpec((TQ, GROUP), lambda b, i: (b * nq + i, 0)),
                   pl.BlockSpec((seq, TQ), qcol)),
        out_shape=(jax.ShapeDtypeStruct((batch * seq, GROUP), bf16),
                   jax.ShapeDtypeStruct((seq, batch * seq), bf16)),
        scratch_shapes=[
            pltpu.VMEM((seq, TQ), f32),
            pltpu.VMEM((IDX_HEADS, LANES, TQ), bf16),
            pltpu.VMEM((n_sm, LANES, TQ), bf16),
        ] + _softmax_scratch(n_sm, 2 * DIFF_DIM),
        compiler_params=pltpu.CompilerParams(
            dimension_semantics=("arbitrary", "arbitrary"), vmem_limit_bytes=56 * MIB),
        name="seldiff",
    )(lq1, lk1, lq2, lk2, g, qit, wit, ki, qdt, kd, vdt)


def _dsa_kernel(qat_ref, bias_ref, ka_ref, vat_ref, o_ref,
                bias_sc, qza_sc, s_sc, p_sc, a_sc, m_sc, acc_sc):
    q0 = pl.program_id(1) * TQ
    n_units = (q0 + TQ + TKA - 1) // TKA
    _split_pairs_t(qat_ref, qza_sc, DSA_HEADS // 2)
    _init_softmax(DSA_HEADS, m_sc, acc_sc)

    def attn_body(c, carry):
        k0 = pl.multiple_of(c * TKA, TKA)
        bias_sc[...] = bias_ref[pl.ds(k0, TKA), :].astype(f32)

        def score_fn(h, r, rows):
            pair = slice((h // 2) * LANES, (h // 2 + 1) * LANES)
            s = jnp.dot(ka_ref[pl.ds(k0 + r, rows), pair], qza_sc[h], preferred_element_type=f32)
            return s + bias_sc[r:r + rows, :]

        def value_fn(h):
            return vat_ref[h * HEAD_DIM:(h + 1) * HEAD_DIM, pl.ds(k0, TKA)]

        _softmax_chunk(DSA_HEADS, score_fn, value_fn, s_sc, p_sc, a_sc, m_sc, acc_sc, s_rows=DSA_S_ROWS)
        return carry

    lax.fori_loop(0, n_units, attn_body, 0)

    ot = jnp.concatenate([_normalized(acc_sc, h, HEAD_DIM) for h in range(DSA_HEADS)], axis=0)
    o_ref[...] = ot.T.astype(o_ref.dtype)


def _dsa(qat, bias, ka, vat, batch, seq):
    nq = seq // TQ
    qcol = lambda b, i: (0, b * nq + i)
    resident = dict(pipeline_mode=pl.Buffered(1))
    return pl.pallas_call(
        _dsa_kernel,
        grid=(batch, nq),
        in_specs=[
            pl.BlockSpec((GROUP, TQ), qcol),
            pl.BlockSpec((seq, TQ), qcol),
            pl.BlockSpec((seq, GROUP), lambda b, i: (b, 0), **resident),
            pl.BlockSpec((GROUP, seq), lambda b, i: (0, b), **resident),
        ],
        out_specs=pl.BlockSpec((TQ, GROUP), lambda b, i: (b * nq + i, 0)),
        out_shape=jax.ShapeDtypeStruct((batch * seq, GROUP), bf16),
        scratch_shapes=[
            pltpu.VMEM((TKA, TQ), f32),
            pltpu.VMEM((DSA_HEADS, LANES, TQ), bf16),
        ] + _softmax_scratch(DSA_HEADS, HEAD_DIM),
        compiler_params=pltpu.CompilerParams(
            dimension_semantics=("arbitrary", "arbitrary"), vmem_limit_bytes=48 * MIB),
        name="dsa",
    )(qat, bias, ka, vat)


def _post_kernel(x_ref, oa_ref, od_ref, wo_ref, wu_ref, wd_ref, g1_ref, g2_ref, g3_ref, o_ref):
    y = (jnp.dot(oa_ref[...], wo_ref[0:DSA_W, :], preferred_element_type=f32)
         + jnp.dot(od_ref[...], wo_ref[DSA_W:DSA_W + DIFF_W, :], preferred_element_type=f32))
    x1 = x_ref[...] + _rms(y) * g1_ref[...]
    h = (_rms(x1) * g2_ref[...]).astype(bf16)
    acc = jnp.zeros_like(x1)
    for c in range(D_FF // FF_CHUNK):
        cols = slice(c * FF_CHUNK, (c + 1) * FF_CHUNK)
        u = jnp.maximum(jnp.dot(h, wu_ref[:, cols], preferred_element_type=f32), 0.0)
        acc = acc + jnp.dot((u * u).astype(bf16), wd_ref[cols, :], preferred_element_type=f32)
    o_ref[...] = x1 + _rms(acc) * g3_ref[...]


def _post(x2d, oa, od, wo, wu, wd, g1, g2, g3):
    m = x2d.shape[0]
    tm = PROJ_TM
    row = lambda i: (i, 0)
    const = lambda i: (0, 0)
    resident = dict(pipeline_mode=pl.Buffered(1))
    gspec = pl.BlockSpec((1, D_MODEL), const)
    return pl.pallas_call(
        _post_kernel,
        grid=(m // tm,),
        in_specs=[
            pl.BlockSpec((tm, D_MODEL), row),
            pl.BlockSpec((tm, DSA_W), row),
            pl.BlockSpec((tm, DIFF_W), row),
            pl.BlockSpec((DSA_W + DIFF_W, D_MODEL), const, **resident),
            pl.BlockSpec((D_MODEL, D_FF), const, **resident),
            pl.BlockSpec((D_FF, D_MODEL), const, **resident),
            gspec, gspec, gspec,
        ],
        out_specs=pl.BlockSpec((tm, D_MODEL), row),
        out_shape=jax.ShapeDtypeStruct((m, D_MODEL), f32),
        compiler_params=pltpu.CompilerParams(
            dimension_semantics=("arbitrary",), vmem_limit_bytes=56 * MIB),
        name="post",
    )(x2d, oa, od, wo, wu, wd, g1, g2, g3)


def _rope_tables(seq):
    inv = 1.0 / (ROPE_THETA ** (jnp.arange(0, HEAD_DIM, 2, dtype=f32) / HEAD_DIM))
    ang = jnp.arange(seq, dtype=f32)[:, None] * inv[None, :]
    cos, sin = jnp.cos(ang), jnp.sin(ang)
    cs = jnp.concatenate([cos, cos, cos, cos], axis=1)
    sn = jnp.concatenate([-sin, sin, -sin, sin], axis=1)
    return cs, sn, cos.T, sin.T


def _prep_w_in(w):
    q_a, k_a, v_a, q_i, k_i, w_i, q_d, k_d, v_d = jnp.split(w, SPLIT_POINTS, axis=1)
    wn = jnp.concatenate([k_a, k_d, k_i, k_i], axis=1)
    pad = jnp.zeros((D_MODEL, WI_ROWS - IDX_HEADS), w.dtype)
    wt = jnp.concatenate([q_a * SM_SCALE, q_i, q_d * SM_SCALE, v_a, v_d, w_i, pad], axis=1).T
    return wn.astype(bf16), wt.astype(bf16)


def kernel(x, w_in, w_out, w_up, w_down, g_pre_mix, g_post_mix, g_pre_mlp, g_post_mlp,
           g_diff_sub, lambda_q1, lambda_k1, lambda_q2, lambda_k2):
    b, s, d = x.shape
    depth = w_in.shape[0]
    assert d == D_MODEL and s % TKA == 0 and s % PROJ_TM == 0 and min(TOPK_MAX, s // 4) == TOPK_MAX
    cs, sn, cst, snt = _rope_tables(s)
    x2d = x.reshape(b * s, d)
    for layer in range(depth):
        wn, wt = _prep_w_in(w_in[layer])
        ka, kd, ki, qat, qit, qdt, vat, vdt, wit = _proj(
            x2d, g_pre_mix[layer][None, :], cs, sn, cst, snt, wn, wt, s)
        lam_init = 0.8 - 0.6 * math.exp(-0.3 * layer)
        od, bias = _seldiff(lambda_q1[layer][None, :], lambda_k1[layer][None, :],
                            lambda_q2[layer][None, :], lambda_k2[layer][None, :],
                            g_diff_sub[layer][None, :], qit, wit, ki, qdt, kd, vdt, lam_init, b, s)
        oa = _dsa(qat, bias, ka, vat, b, s)
        x2d = _post(x2d, oa, od,
                    w_out[layer].astype(bf16), w_up[layer].astype(bf16), w_down[layer].astype(bf16),
                    g_post_mix[layer][None, :], g_pre_mlp[layer][None, :], g_post_mlp[layer][None, :])
    return x2d.reshape(b, s, d)
```

```python
import functools
import math

import jax
import jax.numpy as jnp
import numpy as np
from jax import lax
from jax.experimental import pallas as pl
from jax.experimental.pallas import tpu as pltpu

f32 = jnp.float32
bf16 = jnp.bfloat16
i32 = jnp.int32

D_MODEL = 1024
HEAD_DIM = 64
DSA_HEADS = 8
DIFF_HEADS = 4
DIFF_DIM = 64
IDX_HEADS = 8
IDX_DIM = 64
TOPK_MAX = 256
D_FF = 4 * D_MODEL
ROPE_THETA = 10000.0
EPS = 1e-6
NEG = -1e30

DSA_W = DSA_HEADS * HEAD_DIM
DIFF_W = DIFF_HEADS * 2 * DIFF_DIM
IN_SIZES = (DSA_W, DSA_W, DSA_W, IDX_HEADS * IDX_DIM, IDX_DIM, IDX_HEADS, DIFF_W, DIFF_W, DIFF_W)
SPLIT_POINTS = tuple(int(v) for v in np.cumsum(IN_SIZES)[:-1])
HALF = HEAD_DIM // 2

LANES = 128
SUBLANES = 8
BF16_ROWS = 16
MIB = 1024 * 1024

GROUP = 512
PROJ_TM = 512
WI_ROWS = BF16_ROWS
T_ROWS = 5 * GROUP + WI_ROWS
N_COLS = 2 * GROUP + LANES
FF_CHUNK = 1024

TQ = 256
TKI = 256
TKA = 512
SEL_PASSES = 32
SEL_FILL = (2, 1, 1)
CNT_ROWS = 32
CNT_ACC = 8
DSA_S_ROWS = 256
SUM_ROWS = BF16_ROWS
SM_SCALE = HEAD_DIM ** -0.5 * math.log2(math.e)

_NT = (((1,), (1,)), ((), ()))


def _rms(x):
    return x * lax.rsqrt(jnp.mean(x * x, axis=-1, keepdims=True) + EPS)


def _proj_kernel(x_ref, g_ref, cs_ref, sn_ref, cst_ref, snt_ref, wn_ref, wt_ref,
                 ka_ref, kd_ref, ki_ref, qat_ref, qit_ref, qdt_ref, vat_ref, vdt_ref, wit_ref):
    h = (_rms(x_ref[...]) * g_ref[...]).astype(bf16)
    tm = h.shape[0]

    cs = cs_ref[...]
    sn = sn_ref[...]

    def rope_rows(col0, width, out_ref):
        y = jnp.dot(h, wn_ref[:, col0:col0 + width], preferred_element_type=f32)
        lane = lax.broadcasted_iota(i32, (tm, width), 1)
        partner = jnp.where((lane & (HEAD_DIM - 1)) < HALF,
                            pltpu.roll(y, width - HALF, 1), pltpu.roll(y, HALF, 1))
        reps = width // LANES
        c = cs if reps == 1 else jnp.concatenate([cs] * reps, axis=1)
        s = sn if reps == 1 else jnp.concatenate([sn] * reps, axis=1)
        out_ref[...] = (y * c + partner * s).astype(out_ref.dtype)

    rope_rows(0, GROUP, ka_ref)
    rope_rows(GROUP, GROUP, kd_ref)
    rope_rows(2 * GROUP, LANES, ki_ref)

    yt = lax.dot_general(wt_ref[...], h, _NT, preferred_element_type=f32)
    ct = cst_ref[...]
    st = snt_ref[...]

    def rope_cols(row0, out_ref):
        for hd in range(GROUP // HEAD_DIM):
            r = row0 + hd * HEAD_DIM
            x1 = yt[r:r + HALF]
            x2 = yt[r + HALF:r + HEAD_DIM]
            out_ref[hd * HEAD_DIM:hd * HEAD_DIM + HALF, :] = (x1 * ct - x2 * st).astype(out_ref.dtype)
            out_ref[hd * HEAD_DIM + HALF:(hd + 1) * HEAD_DIM, :] = (x2 * ct + x1 * st).astype(out_ref.dtype)

    rope_cols(0, qat_ref)
    rope_cols(GROUP, qit_ref)
    rope_cols(2 * GROUP, qdt_ref)
    vat_ref[...] = yt[3 * GROUP:4 * GROUP].astype(bf16)
    vdt_ref[...] = yt[4 * GROUP:5 * GROUP].astype(bf16)
    idx_scale = (IDX_HEADS ** -0.5) * (IDX_DIM ** -0.5)
    wit_ref[...] = yt[5 * GROUP:5 * GROUP + WI_ROWS] * idx_scale


def _proj(x2d, g, cs, sn, cst, snt, wn, wt, seq):
    m = x2d.shape[0]
    tm = PROJ_TM
    pos_blocks = seq // tm
    row = lambda i: (i, 0)
    col = lambda i: (0, i)
    const = lambda i: (0, 0)
    rows_out = jax.ShapeDtypeStruct((m, GROUP), bf16)
    cols_out = jax.ShapeDtypeStruct((GROUP, m), bf16)
    out_shape = (rows_out, rows_out, jax.ShapeDtypeStruct((m, LANES), bf16),
                 cols_out, cols_out, cols_out, cols_out, cols_out,
                 jax.ShapeDtypeStruct((WI_ROWS, m), f32))
    rows_spec = pl.BlockSpec((tm, GROUP), row)
    cols_spec = pl.BlockSpec((GROUP, tm), col)
    return pl.pallas_call(
        _proj_kernel,
        grid=(m // tm,),
        in_specs=[
            pl.BlockSpec((tm, D_MODEL), row),
            pl.BlockSpec((1, D_MODEL), const),
            pl.BlockSpec((tm, LANES), lambda i: (i % pos_blocks, 0)),
            pl.BlockSpec((tm, LANES), lambda i: (i % pos_blocks, 0)),
            pl.BlockSpec((HALF, tm), lambda i: (0, i % pos_blocks)),
            pl.BlockSpec((HALF, tm), lambda i: (0, i % pos_blocks)),
            pl.BlockSpec((D_MODEL, N_COLS), const),
            pl.BlockSpec((T_ROWS, D_MODEL), const),
        ],
        out_specs=(rows_spec, rows_spec, pl.BlockSpec((tm, LANES), row),
                   cols_spec, cols_spec, cols_spec, cols_spec, cols_spec,
                   pl.BlockSpec((WI_ROWS, tm), col)),
        out_shape=out_shape,
        compiler_params=pltpu.CompilerParams(
            dimension_semantics=("arbitrary",), vmem_limit_bytes=56 * MIB),
        name="proj",
    )(x2d, g, cs, sn, cst, snt, wn, wt)


def _split_pairs_t(src_ref, dst_sc, n_pairs):
    zero = jnp.zeros((HEAD_DIM, dst_sc.shape[2]), dst_sc.dtype)
    for p in range(n_pairs):
        r = p * LANES
        dst_sc[2 * p, 0:HEAD_DIM, :] = src_ref[r:r + HEAD_DIM, :]
        dst_sc[2 * p, HEAD_DIM:LANES, :] = zero
        dst_sc[2 * p + 1, 0:HEAD_DIM, :] = zero
        dst_sc[2 * p + 1, HEAD_DIM:LANES, :] = src_ref[r + HEAD_DIM:r + LANES, :]


def _softmax_chunk(n, score_fn, value_fn, s_sc, p_sc, a_sc, m_sc, acc_sc, filler=None, s_rows=TKA):
    ones = jnp.ones((SUM_ROWS, TKA), bf16)
    fill = filler if filler is not None else (lambda phase: None)
    for j in range(n):
        mx = None
        for r in range(0, TKA, s_rows):
            s = score_fn(j, r, s_rows)
            s_sc[j, r:r + s_rows, :] = s
            bm = jnp.max(s, axis=0, keepdims=True)
            mx = bm if mx is None else jnp.maximum(mx, bm)
        a_sc[j] = mx
        fill(0)
    for j in range(n):
        m_old = m_sc[j]
        m_new = jnp.maximum(m_old, a_sc[j])
        p_sc[j] = jnp.exp2(s_sc[j] - m_new).astype(bf16)
        a_sc[j] = jnp.exp2(m_old - m_new)
        m_sc[j] = m_new
        fill(1)
    for j in range(n):
        vt = jnp.concatenate([value_fn(j), ones], axis=0)
        acc_sc[j] = a_sc[j] * acc_sc[j] + jnp.dot(vt, p_sc[j], preferred_element_type=f32)
        fill(2)


def _init_softmax(n, m_sc, acc_sc):
    for j in range(n):
        m_sc[j] = jnp.full(m_sc.shape[1:], NEG, f32)
        acc_sc[j] = jnp.zeros(acc_sc.shape[1:], f32)


def _normalized(acc_sc, j, dv):
    return acc_sc[j, 0:dv, :] / acc_sc[j, dv:dv + 1, :]


def _softmax_scratch(n, dv):
    return [
        pltpu.VMEM((n, TKA, TQ), f32),
        pltpu.VMEM((n, TKA, TQ), bf16),
        pltpu.VMEM((n, 1, TQ), f32),
        pltpu.VMEM((n, 1, TQ), f32),
        pltpu.VMEM((n, dv + SUM_ROWS, TQ), f32),
    ]


def _key_to_float(key):
    return pltpu.bitcast(key ^ ((key >> 31) & jnp.int32(0x7FFFFFFF)), f32)


def _sc_rows(sc_sc, base, rows, block=None):
    if block is not None:
        return sc_sc[block, pl.ds(base, rows), :]
    return jnp.concatenate([sc_sc[b, pl.ds(base, rows), :] for b in range(sc_sc.shape[0])], axis=1)


def _sc_store(sc_sc, base, rows, val):
    for b in range(sc_sc.shape[0]):
        sc_sc[b, pl.ds(base, rows), :] = val[:, b * LANES:(b + 1) * LANES]


def _unit_count(sc_sc, u, pred, block=None):
    base = pl.multiple_of(u * TKA, TKA)
    width = LANES if block is not None else sc_sc.shape[0] * LANES
    shape = (CNT_ROWS // SUBLANES, SUBLANES, width)
    sub = (lax.broadcasted_iota(i32, shape, 0) * SUBLANES + lax.broadcasted_iota(i32, shape, 1))
    parts = [jnp.zeros((SUBLANES, width), f32) for _ in range(CNT_ACC)]
    for j in range(TKA // CNT_ROWS):
        blk = _sc_rows(sc_sc, base + j * CNT_ROWS, CNT_ROWS, block).reshape(shape)
        ind = jnp.where(pred(blk, base + j * CNT_ROWS + sub), 1.0, 0.0).astype(f32)
        parts[j % CNT_ACC] = parts[j % CNT_ACC] + jnp.sum(ind, axis=0)
    return functools.reduce(lambda a, b: a + b, parts)


def _count_total(acc):
    return jnp.broadcast_to(jnp.sum(acc, axis=0, keepdims=True), acc.shape)


def _seldiff_kernel(lq1_ref, lk1_ref, lq2_ref, lk2_ref, g_ref, qit_ref, wit_ref, ki_ref,
                    qdt_ref, kd_ref, vdt_ref, od_ref, bias_ref,
                    sc_sc, qzi_sc, qz_sc, s_sc, p_sc, a_sc, m_sc, acc_sc, *, lam_init):
    q0 = pl.program_id(1) * TQ
    n_units = (q0 + TQ + TKA - 1) // TKA
    n_sm = 2 * DIFF_HEADS
    neg_inf = jnp.float32(-jnp.inf)

    _split_pairs_t(qit_ref, qzi_sc, IDX_HEADS // 2)
    _split_pairs_t(qdt_ref, qz_sc, DIFF_HEADS)
    _init_softmax(n_sm, m_sc, acc_sc)

    key_a = lax.broadcasted_iota(i32, (TKI, TQ), 0)
    qry_a = q0 + lax.broadcasted_iota(i32, (TKI, TQ), 1)

    def score_body(c, carry):
        k0 = pl.multiple_of(c * TKI, TKI)
        kc = ki_ref[pl.ds(k0, TKI), :]
        score = jnp.zeros((TKI, TQ), f32)
        for h in range(IDX_HEADS):
            logit = jnp.dot(kc, qzi_sc[h], preferred_element_type=f32)
            score = score + wit_ref[h:h + 1, :] * jnp.maximum(logit, 0.0)
        _sc_store(sc_sc, k0, TKI, jnp.where(key_a + k0 <= qry_a, score, neg_inf))
        return carry

    lax.fori_loop(0, n_units * (TKA // TKI), score_body, 0)

    qpos = q0 + lax.broadcasted_iota(i32, (SUBLANES, TQ), 1)
    k_sel = jnp.minimum(qpos + 1, TOPK_MAX).astype(f32)

    lane_blocks = [slice(b * LANES, (b + 1) * LANES) for b in range(TQ // LANES)]

    def select_step(state):
        p, u, blocks = state
        bit = lax.shift_left(jnp.int32(1), jnp.int32(31) - p)
        last = u == n_units - 1
        new_blocks = []
        for b, (lanes, (t_key, acc, n_ge)) in enumerate(zip(lane_blocks, blocks)):
            cand_key = t_key + bit
            cand = _key_to_float(cand_key)
            acc = acc + _unit_count(sc_sc, u, lambda blk, idx: blk >= cand[None], b)
            cnt = _count_total(acc)
            take = jnp.logical_and(last, cnt >= k_sel[:, lanes])
            new_blocks.append((jnp.where(take, cand_key, t_key),
                               jnp.where(last, jnp.zeros_like(acc), acc),
                               jnp.where(take, cnt, n_ge)))
        return jnp.where(last, p + 1, p), jnp.where(last, 0, u + 1), tuple(new_blocks)

    def chunk(c, masked, state):
        k0 = pl.multiple_of(c * TKA, TKA)
        box = [state]

        def filler(phase):
            for _ in range(SEL_FILL[phase]):
                box[0] = select_step(box[0])

        if masked:
            causal = (k0 + lax.broadcasted_iota(i32, (TKA, TQ), 0)
                      <= q0 + lax.broadcasted_iota(i32, (TKA, TQ), 1))

        def score_fn(j, r, rows):
            h = j // 2
            s = jnp.dot(kd_ref[pl.ds(k0 + r, rows), h * LANES:(h + 1) * LANES], qz_sc[j],
                        preferred_element_type=f32)
            return jnp.where(causal[r:r + rows], s, NEG) if masked else s

        def value_fn(j):
            h = j // 2
            return vdt_ref[h * LANES:(h + 1) * LANES, pl.ds(k0, TKA)]

        _softmax_chunk(n_sm, score_fn, value_fn, s_sc, p_sc, a_sc, m_sc, acc_sc, filler)
        return box[0]

    assert n_sm * sum(SEL_FILL) == SEL_PASSES
    n_causal = (qpos + 1).astype(f32)
    state = (jnp.int32(0), jnp.int32(0),
             tuple((jnp.full((SUBLANES, LANES), -(2 ** 31), i32), jnp.zeros((SUBLANES, LANES), f32),
                    n_causal[:, lanes]) for lanes in lane_blocks))
    state = lax.fori_loop(0, n_units - 1, lambda c, st: chunk(c, False, st), state)
    _, _, blocks = chunk(n_units - 1, True, state)
    thr = _key_to_float(jnp.concatenate([blk[0] for blk in blocks], axis=1))
    n_ge = jnp.concatenate([blk[2] for blk in blocks], axis=1)

    def count(pred):
        acc = lax.fori_loop(0, n_units, lambda u, acc: acc + _unit_count(sc_sc, u, pred),
                            jnp.zeros((SUBLANES, TQ), f32))
        return _count_total(acc)

    @pl.when(jnp.max(n_ge - k_sel) > 0.0)
    def _():
        need = k_sel - count(lambda blk, idx: blk > thr[None])
        idx_bits = (sc_sc.shape[1] - 1).bit_length()

        def idx_body(b, j):
            cand = j + lax.shift_left(jnp.int32(1), jnp.int32(idx_bits - 1) - b)
            cnt = count(lambda blk, idx: (blk == thr[None]) & (idx < cand[None]))
            return jnp.where(cnt < need, cand, j)

        j_last = lax.fori_loop(0, idx_bits, idx_body, jnp.zeros((SUBLANES, TQ), i32))

        def fix_body(u, carry):
            base = pl.multiple_of(u * TKA, TKA)
            blk = _sc_rows(sc_sc, base, TKA)
            idx = base + lax.broadcasted_iota(i32, blk.shape, 0)
            drop = (blk == thr[0:1]) & (idx > j_last[0:1])
            _sc_store(sc_sc, base, TKA, jnp.where(drop, neg_inf, blk))
            return carry

        lax.fori_loop(0, n_units, fix_body, 0)

    def bias_body(u, carry):
        base = pl.multiple_of(u * TKA, TKA)
        keep = _sc_rows(sc_sc, base, TKA) >= thr[0:1]
        bias_ref[pl.ds(base, TKA), :] = jnp.where(keep, 0.0, NEG).astype(bias_ref.dtype)
        return carry

    def fill_body(u, carry):
        base = pl.multiple_of(u * TKA, TKA)
        bias_ref[pl.ds(base, TKA), :] = jnp.full((TKA, TQ), NEG, bias_ref.dtype)
        return carry

    lax.fori_loop(0, n_units, bias_body, 0)
    lax.fori_loop(n_units, sc_sc.shape[1] // TKA, fill_body, 0)

    lam = (jnp.exp(jnp.sum(lq1_ref[...] * lk1_ref[...], axis=1, keepdims=True))
           - jnp.exp(jnp.sum(lq2_ref[...] * lk2_ref[...], axis=1, keepdims=True)) + lam_init)
    heads = []
    for h in range(DIFF_HEADS):
        o = (_normalized(acc_sc, 2 * h, 2 * DIFF_DIM)
             - lam * _normalized(acc_sc, 2 * h + 1, 2 * DIFF_DIM))
        heads.append(o * lax.rsqrt(jnp.mean(o * o, axis=0, keepdims=True) + EPS))
    o = jnp.concatenate(heads, axis=0).T
    g = g_ref[...]
    od_ref[...] = (o * jnp.concatenate([g] * DIFF_HEADS, axis=1) * (1.0 - lam_init)).astype(od_ref.dtype)


def _seldiff(lq1, lk1, lq2, lk2, g, qit, wit, ki, qdt, kd, vdt, lam_init, batch, seq):
    assert TKA == 2 * TQ
    nq = seq // TQ
    const = lambda b, i: (0, 0)
    qcol = lambda b, i: (0, b * nq + i)
    resident = dict(pipeline_mode=pl.Buffered(1))
    n_sm = 2 * DIFF_HEADS
    vec = pl.BlockSpec((1, DIFF_DIM), const)
    return pl.pallas_call(
        functools.partial(_seldiff_kernel, lam_init=lam_init),
        grid=(batch, nq),
        in_specs=[
            vec, vec, vec, vec,
            pl.BlockSpec((1, 2 * DIFF_DIM), const),
            pl.BlockSpec((GROUP, TQ), qcol),
            pl.BlockSpec((WI_ROWS, TQ), qcol),
            pl.BlockSpec((seq, LANES), lambda b, i: (b, 0), **resident),
            pl.BlockSpec((GROUP, TQ), qcol),
            pl.BlockSpec((seq, GROUP), lambda b, i: (b, 0), **resident),
            pl.BlockSpec((GROUP, seq), lambda b, i: (0, b), **resident),
        ],
        out_specs=(pl.BlockSpec((TQ, GROUP), lambda b, i: (b * nq + i, 0)),
                   pl.BlockSpec((seq, TQ), qcol)),
        out_shape=(jax.ShapeDtypeStruct((batch * seq, GROUP), bf16),
                   jax.ShapeDtypeStruct((seq, batch * seq), bf16)),
        scratch_shapes=[
            pltpu.VMEM((TQ // LANES, seq, LANES), f32),
            pltpu.VMEM((IDX_HEADS, LANES, TQ), bf16),
            pltpu.VMEM((n_sm, LANES, TQ), bf16),
        ] + _softmax_scratch(n_sm, 2 * DIFF_DIM),
        compiler_params=pltpu.CompilerParams(
            dimension_semantics=("arbitrary", "arbitrary"), vmem_limit_bytes=56 * MIB),
        name="seldiff",
    )(lq1, lk1, lq2, lk2, g, qit, wit, ki, qdt, kd, vdt)


def _dsa_kernel(qat_ref, bias_ref, ka_ref, vat_ref, o_ref,
                bias_sc, qza_sc, s_sc, p_sc, a_sc, m_sc, acc_sc):
    q0 = pl.program_id(1) * TQ
    n_units = (q0 + TQ + TKA - 1) // TKA
    _split_pairs_t(qat_ref, qza_sc, DSA_HEADS // 2)
    _init_softmax(DSA_HEADS, m_sc, acc_sc)

    def attn_body(c, carry):
        k0 = pl.multiple_of(c * TKA, TKA)
        bias_sc[...] = bias_ref[pl.ds(k0, TKA), :].astype(f32)

        def score_fn(h, r, rows):
            pair = slice((h // 2) * LANES, (h // 2 + 1) * LANES)
            s = jnp.dot(ka_ref[pl.ds(k0 + r, rows), pair], qza_sc[h], preferred_element_type=f32)
            return s + bias_sc[r:r + rows, :]

        def value_fn(h):
            return vat_ref[h * HEAD_DIM:(h + 1) * HEAD_DIM, pl.ds(k0, TKA)]

        _softmax_chunk(DSA_HEADS, score_fn, value_fn, s_sc, p_sc, a_sc, m_sc, acc_sc, s_rows=DSA_S_ROWS)
        return carry

    lax.fori_loop(0, n_units, attn_body, 0)

    ot = jnp.concatenate([_normalized(acc_sc, h, HEAD_DIM) for h in range(DSA_HEADS)], axis=0)
    o_ref[...] = ot.T.astype(o_ref.dtype)


def _dsa(qat, bias, ka, vat, batch, seq):
    nq = seq // TQ
    qcol = lambda b, i: (0, b * nq + i)
    resident = dict(pipeline_mode=pl.Buffered(1))
    return pl.pallas_call(
        _dsa_kernel,
        grid=(batch, nq),
        in_specs=[
            pl.BlockSpec((GROUP, TQ), qcol),
            pl.BlockSpec((seq, TQ), qcol),
            pl.BlockSpec((seq, GROUP), lambda b, i: (b, 0), **resident),
            pl.BlockSpec((GROUP, seq), lambda b, i: (0, b), **resident),
        ],
        out_specs=pl.BlockSpec((TQ, GROUP), lambda b, i: (b * nq + i, 0)),
        out_shape=jax.ShapeDtypeStruct((batch * seq, GROUP), bf16),
        scratch_shapes=[
            pltpu.VMEM((TKA, TQ), f32),
            pltpu.VMEM((DSA_HEADS, LANES, TQ), bf16),
        ] + _softmax_scratch(DSA_HEADS, HEAD_DIM),
        compiler_params=pltpu.CompilerParams(
            dimension_semantics=("arbitrary", "arbitrary"), vmem_limit_bytes=48 * MIB),
        name="dsa",
    )(qat, bias, ka, vat)


def _post_kernel(x_ref, oa_ref, od_ref, wo_ref, wu_ref, wd_ref, g1_ref, g2_ref, g3_ref, o_ref):
    y = (jnp.dot(oa_ref[...], wo_ref[0:DSA_W, :], preferred_element_type=f32)
         + jnp.dot(od_ref[...], wo_ref[DSA_W:DSA_W + DIFF_W, :], preferred_element_type=f32))
    x1 = x_ref[...] + _rms(y) * g1_ref[...]
    h = (_rms(x1) * g2_ref[...]).astype(bf16)
    acc = jnp.zeros_like(x1)
    for c in range(D_FF // FF_CHUNK):
        cols = slice(c * FF_CHUNK, (c + 1) * FF_CHUNK)
        u = jnp.maximum(jnp.dot(h, wu_ref[:, cols], preferred_element_type=f32), 0.0)
        acc = acc + jnp.dot((u * u).astype(bf16), wd_ref[cols, :], preferred_element_type=f32)
    o_ref[...] = x1 + _rms(acc) * g3_ref[...]


def _post(x2d, oa, od, wo, wu, wd, g1, g2, g3):
    m = x2d.shape[0]
    tm = PROJ_TM
    row = lambda i: (i, 0)
    const = lambda i: (0, 0)
    resident = dict(pipeline_mode=pl.Buffered(1))
    gspec = pl.BlockSpec((1, D_MODEL), const)
    return pl.pallas_call(
        _post_kernel,
        grid=(m // tm,),
        in_specs=[
            pl.BlockSpec((tm, D_MODEL), row),
            pl.BlockSpec((tm, DSA_W), row),
            pl.BlockSpec((tm, DIFF_W), row),
            pl.BlockSpec((DSA_W + DIFF_W, D_MODEL), const, **resident),
            pl.BlockSpec((D_MODEL, D_FF), const, **resident),
            pl.BlockSpec((D_FF, D_MODEL), const, **resident),
            gspec, gspec, gspec,
        ],
        out_specs=pl.BlockSpec((tm, D_MODEL), row),
        out_shape=jax.ShapeDtypeStruct((m, D_MODEL), f32),
        compiler_params=pltpu.CompilerParams(
            dimension_semantics=("arbitrary",), vmem_limit_bytes=56 * MIB),
        name="post",
    )(x2d, oa, od, wo, wu, wd, g1, g2, g3)


def _rope_tables(seq):
    inv = 1.0 / (ROPE_THETA ** (jnp.arange(0, HEAD_DIM, 2, dtype=f32) / HEAD_DIM))
    ang = jnp.arange(seq, dtype=f32)[:, None] * inv[None, :]
    cos, sin = jnp.cos(ang), jnp.sin(ang)
    cs = jnp.concatenate([cos, cos, cos, cos], axis=1)
    sn = jnp.concatenate([-sin, sin, -sin, sin], axis=1)
    return cs, sn, cos.T, sin.T


def _prep_w_in(w):
    q_a, k_a, v_a, q_i, k_i, w_i, q_d, k_d, v_d = jnp.split(w, SPLIT_POINTS, axis=1)
    wn = jnp.concatenate([k_a, k_d, k_i, k_i], axis=1)
    pad = jnp.zeros((D_MODEL, WI_ROWS - IDX_HEADS), w.dtype)
    wt = jnp.concatenate([q_a * SM_SCALE, q_i, q_d * SM_SCALE, v_a, v_d, w_i, pad], axis=1).T
    return wn.astype(bf16), wt.astype(bf16)


def kernel(x, w_in, w_out, w_up, w_down, g_pre_mix, g_post_mix, g_pre_mlp, g_post_mlp,
           g_diff_sub, lambda_q1, lambda_k1, lambda_q2, lambda_k2):
    b, s, d = x.shape
    depth = w_in.shape[0]
    assert d == D_MODEL and s % TKA == 0 and s % PROJ_TM == 0 and min(TOPK_MAX, s // 4) == TOPK_MAX
    cs, sn, cst, snt = _rope_tables(s)
    x2d = x.reshape(b * s, d)
    for layer in range(depth):
        wn, wt = _prep_w_in(w_in[layer])
        ka, kd, ki, qat, qit, qdt, vat, vdt, wit = _proj(
            x2d, g_pre_mix[layer][None, :], cs, sn, cst, snt, wn, wt, s)
        lam_init = 0.8 - 0.6 * math.exp(-0.3 * layer)
        od, bias = _seldiff(lambda_q1[layer][None, :], lambda_k1[layer][None, :],
                            lambda_q2[layer][None, :], lambda_k2[layer][None, :],
                            g_diff_sub[layer][None, :], qit, wit, ki, qdt, kd, vdt, lam_init, b, s)
        oa = _dsa(qat, bias, ka, vat, b, s)
        x2d = _post(x2d, oa, od,
                    w_out[layer].astype(bf16), w_up[layer].astype(bf16), w_down[layer].astype(bf16),
                    g_post_mix[layer][None, :], g_pre_mlp[layer][None, :], g_post_mlp[layer][None, :])
    return x2d.reshape(b, s, d)
```

```python
import functools
import math

import jax
import jax.numpy as jnp
import numpy as np
from jax import lax
from jax.experimental import pallas as pl
from jax.experimental.pallas import tpu as pltpu

f32 = jnp.float32
bf16 = jnp.bfloat16
i32 = jnp.int32

D_MODEL = 1024
HEAD_DIM = 64
DSA_HEADS = 8
DIFF_HEADS = 4
DIFF_DIM = 64
IDX_HEADS = 8
IDX_DIM = 64
TOPK_MAX = 256
D_FF = 4 * D_MODEL
ROPE_THETA = 10000.0
EPS = 1e-6
NEG = -1e30

DSA_W = DSA_HEADS * HEAD_DIM
DIFF_W = DIFF_HEADS * 2 * DIFF_DIM
IN_SIZES = (DSA_W, DSA_W, DSA_W, IDX_HEADS * IDX_DIM, IDX_DIM, IDX_HEADS, DIFF_W, DIFF_W, DIFF_W)
SPLIT_POINTS = tuple(int(v) for v in np.cumsum(IN_SIZES)[:-1])
HALF = HEAD_DIM // 2

LANES = 128
SUBLANES = 8
BF16_ROWS = 16
MIB = 1024 * 1024

GROUP = 512
PROJ_TM = 512
WI_ROWS = BF16_ROWS
T_ROWS = 5 * GROUP + WI_ROWS
N_COLS = 2 * GROUP + LANES
FF_CHUNK = 1024

TQ = 256
TKI = 512
TKA = 512
SEL_PASSES = 32
SEL_FILL = (2, 1, 1)
CNT_ROWS = 32
CNT_ACC = 8
DSA_S_ROWS = 256
SUM_ROWS = BF16_ROWS
SM_SCALE = HEAD_DIM ** -0.5 * math.log2(math.e)

_NT = (((1,), (1,)), ((), ()))


def _rms(x):
    return x * lax.rsqrt(jnp.mean(x * x, axis=-1, keepdims=True) + EPS)


def _proj_kernel(x_ref, g_ref, cs_ref, sn_ref, cst_ref, snt_ref, wn_ref, wt_ref,
                 ka_ref, kd_ref, ki_ref, qat_ref, qit_ref, qdt_ref, vat_ref, vdt_ref, wit_ref):
    h = (_rms(x_ref[...]) * g_ref[...]).astype(bf16)
    tm = h.shape[0]

    cs = cs_ref[...]
    sn = sn_ref[...]

    def rope_rows(col0, width, out_ref):
        y = jnp.dot(h, wn_ref[:, col0:col0 + width], preferred_element_type=f32)
        lane = lax.broadcasted_iota(i32, (tm, width), 1)
        partner = jnp.where((lane & (HEAD_DIM - 1)) < HALF,
                            pltpu.roll(y, width - HALF, 1), pltpu.roll(y, HALF, 1))
        reps = width // LANES
        c = cs if reps == 1 else jnp.concatenate([cs] * reps, axis=1)
        s = sn if reps == 1 else jnp.concatenate([sn] * reps, axis=1)
        out_ref[...] = (y * c + partner * s).astype(out_ref.dtype)

    rope_rows(0, GROUP, ka_ref)
    rope_rows(GROUP, GROUP, kd_ref)
    rope_rows(2 * GROUP, LANES, ki_ref)

    yt = lax.dot_general(wt_ref[...], h, _NT, preferred_element_type=f32)
    ct = cst_ref[...]
    st = snt_ref[...]

    def rope_cols(row0, out_ref):
        for hd in range(GROUP // HEAD_DIM):
            r = row0 + hd * HEAD_DIM
            x1 = yt[r:r + HALF]
            x2 = yt[r + HALF:r + HEAD_DIM]
            out_ref[hd * HEAD_DIM:hd * HEAD_DIM + HALF, :] = (x1 * ct - x2 * st).astype(out_ref.dtype)
            out_ref[hd * HEAD_DIM + HALF:(hd + 1) * HEAD_DIM, :] = (x2 * ct + x1 * st).astype(out_ref.dtype)

    rope_cols(0, qat_ref)
    rope_cols(GROUP, qit_ref)
    rope_cols(2 * GROUP, qdt_ref)
    vat_ref[...] = yt[3 * GROUP:4 * GROUP].astype(bf16)
    vdt_ref[...] = yt[4 * GROUP:5 * GROUP].astype(bf16)
    idx_scale = (IDX_HEADS ** -0.5) * (IDX_DIM ** -0.5)
    wit_ref[...] = yt[5 * GROUP:5 * GROUP + WI_ROWS] * idx_scale


def _proj(x2d, g, cs, sn, cst, snt, wn, wt, seq):
    m = x2d.shape[0]
    tm = PROJ_TM
    pos_blocks = seq // tm
    row = lambda i: (i, 0)
    col = lambda i: (0, i)
    const = lambda i: (0, 0)
    rows_out = jax.ShapeDtypeStruct((m, GROUP), bf16)
    cols_out = jax.ShapeDtypeStruct((GROUP, m), bf16)
    out_shape = (rows_out, rows_out, jax.ShapeDtypeStruct((m, LANES), bf16),
                 cols_out, cols_out, cols_out, cols_out, cols_out,
                 jax.ShapeDtypeStruct((WI_ROWS, m), f32))
    rows_spec = pl.BlockSpec((tm, GROUP), row)
    cols_spec = pl.BlockSpec((GROUP, tm), col)
    return pl.pallas_call(
        _proj_kernel,
        grid=(m // tm,),
        in_specs=[
            pl.BlockSpec((tm, D_MODEL), row),
            pl.BlockSpec((1, D_MODEL), const),
            pl.BlockSpec((tm, LANES), lambda i: (i % pos_blocks, 0)),
            pl.BlockSpec((tm, LANES), lambda i: (i % pos_blocks, 0)),
            pl.BlockSpec((HALF, tm), lambda i: (0, i % pos_blocks)),
            pl.BlockSpec((HALF, tm), lambda i: (0, i % pos_blocks)),
            pl.BlockSpec((D_MODEL, N_COLS), const),
            pl.BlockSpec((T_ROWS, D_MODEL), const),
        ],
        out_specs=(rows_spec, rows_spec, pl.BlockSpec((tm, LANES), row),
                   cols_spec, cols_spec, cols_spec, cols_spec, cols_spec,
                   pl.BlockSpec((WI_ROWS, tm), col)),
        out_shape=out_shape,
        compiler_params=pltpu.CompilerParams(
            dimension_semantics=("arbitrary",), vmem_limit_bytes=56 * MIB),
        name="proj",
    )(x2d, g, cs, sn, cst, snt, wn, wt)


def _split_pairs_t(src_ref, dst_sc, n_pairs):
    zero = jnp.zeros((HEAD_DIM, dst_sc.shape[2]), dst_sc.dtype)
    for p in range(n_pairs):
        r = p * LANES
        dst_sc[2 * p, 0:HEAD_DIM, :] = src_ref[r:r + HEAD_DIM, :]
        dst_sc[2 * p, HEAD_DIM:LANES, :] = zero
        dst_sc[2 * p + 1, 0:HEAD_DIM, :] = zero
        dst_sc[2 * p + 1, HEAD_DIM:LANES, :] = src_ref[r + HEAD_DIM:r + LANES, :]


def _softmax_chunk(n, score_fn, value_fn, s_sc, p_sc, a_sc, m_sc, acc_sc, filler=None, s_rows=TKA):
    ones = jnp.ones((SUM_ROWS, TKA), bf16)
    fill = filler if filler is not None else (lambda phase: None)

    def scores(j):
        mx = None
        for r in range(0, TKA, s_rows):
            s = score_fn(j, r, s_rows)
            s_sc[j, r:r + s_rows, :] = s
            bm = jnp.max(s, axis=0, keepdims=True)
            mx = bm if mx is None else jnp.maximum(mx, bm)
        a_sc[j] = mx
        fill(0)

    def softmax(j):
        m_old = m_sc[j]
        m_new = jnp.maximum(m_old, a_sc[j])
        p_sc[j] = jnp.exp2(s_sc[j] - m_new).astype(bf16)
        a_sc[j] = jnp.exp2(m_old - m_new)
        m_sc[j] = m_new
        fill(1)

    def values(j):
        vt = jnp.concatenate([value_fn(j), ones], axis=0)
        acc_sc[j] = a_sc[j] * acc_sc[j] + jnp.dot(vt, p_sc[j], preferred_element_type=f32)
        fill(2)

    for j in range(n):
        scores(j)
    for j in range(n):
        softmax(j)
    for j in range(n):
        values(j)


def _init_softmax(n, m_sc, acc_sc):
    for j in range(n):
        m_sc[j] = jnp.full(m_sc.shape[1:], NEG, f32)
        acc_sc[j] = jnp.zeros(acc_sc.shape[1:], f32)


def _normalized(acc_sc, j, dv):
    return acc_sc[j, 0:dv, :] / acc_sc[j, dv:dv + 1, :]


def _softmax_scratch(n, dv):
    return [
        pltpu.VMEM((n, TKA, TQ), f32),
        pltpu.VMEM((n, TKA, TQ), bf16),
        pltpu.VMEM((n, 1, TQ), f32),
        pltpu.VMEM((n, 1, TQ), f32),
        pltpu.VMEM((n, dv + SUM_ROWS, TQ), f32),
    ]


def _key_to_float(key):
    return pltpu.bitcast(key ^ ((key >> 31) & jnp.int32(0x7FFFFFFF)), f32)


def _sc_rows(sc_sc, base, rows, block=None):
    if block is not None:
        return sc_sc[block, pl.ds(base, rows), :]
    return jnp.concatenate([sc_sc[b, pl.ds(base, rows), :] for b in range(sc_sc.shape[0])], axis=1)


def _sc_store(sc_sc, base, rows, val):
    for b in range(sc_sc.shape[0]):
        sc_sc[b, pl.ds(base, rows), :] = val[:, b * LANES:(b + 1) * LANES]


def _unit_count(sc_sc, u, pred, block=None):
    base = pl.multiple_of(u * TKA, TKA)
    width = LANES if block is not None else sc_sc.shape[0] * LANES
    shape = (CNT_ROWS // SUBLANES, SUBLANES, width)
    sub = (lax.broadcasted_iota(i32, shape, 0) * SUBLANES + lax.broadcasted_iota(i32, shape, 1))
    parts = [jnp.zeros((SUBLANES, width), f32) for _ in range(CNT_ACC)]
    for j in range(TKA // CNT_ROWS):
        blk = _sc_rows(sc_sc, base + j * CNT_ROWS, CNT_ROWS, block).reshape(shape)
        ind = jnp.where(pred(blk, base + j * CNT_ROWS + sub), 1.0, 0.0).astype(f32)
        parts[j % CNT_ACC] = parts[j % CNT_ACC] + jnp.sum(ind, axis=0)
    return functools.reduce(lambda a, b: a + b, parts)


def _count_total(acc):
    return jnp.broadcast_to(jnp.sum(acc, axis=0, keepdims=True), acc.shape)


def _seldiff_kernel(lq1_ref, lk1_ref, lq2_ref, lk2_ref, g_ref, qit_ref, wit_ref, ki_ref,
                    qdt_ref, kd_ref, vdt_ref, od_ref, bias_ref,
                    sc_sc, qzi_sc, qz_sc, s_sc, p_sc, a_sc, m_sc, acc_sc, *, lam_init):
    q0 = pl.program_id(1) * TQ
    n_units = (q0 + TQ + TKA - 1) // TKA
    n_sm = 2 * DIFF_HEADS
    neg_inf = jnp.float32(-jnp.inf)

    _split_pairs_t(qit_ref, qzi_sc, IDX_HEADS // 2)
    _split_pairs_t(qdt_ref, qz_sc, DIFF_HEADS)
    _init_softmax(n_sm, m_sc, acc_sc)

    key_a = lax.broadcasted_iota(i32, (TKI, TQ), 0)
    qry_a = q0 + lax.broadcasted_iota(i32, (TKI, TQ), 1)

    def score_body(c, carry):
        k0 = pl.multiple_of(c * TKI, TKI)
        kc = ki_ref[pl.ds(k0, TKI), :]
        score = jnp.zeros((TKI, TQ), f32)
        for h in range(IDX_HEADS):
            logit = jnp.dot(kc, qzi_sc[h], preferred_element_type=f32)
            score = score + wit_ref[h:h + 1, :] * jnp.maximum(logit, 0.0)
        _sc_store(sc_sc, k0, TKI, jnp.where(key_a + k0 <= qry_a, score, neg_inf))
        return carry

    lax.fori_loop(0, n_units * (TKA // TKI), score_body, 0)

    qpos = q0 + lax.broadcasted_iota(i32, (SUBLANES, TQ), 1)
    k_sel = jnp.minimum(qpos + 1, TOPK_MAX).astype(f32)

    lane_blocks = [slice(b * LANES, (b + 1) * LANES) for b in range(TQ // LANES)]

    def select_step(state):
        p, u, blocks = state
        bit = lax.shift_left(jnp.int32(1), jnp.int32(31) - p)
        last = u == n_units - 1
        new_blocks = []
        for b, (lanes, (t_key, acc, n_ge)) in enumerate(zip(lane_blocks, blocks)):
            cand_key = t_key + bit
            cand = _key_to_float(cand_key)
            acc = acc + _unit_count(sc_sc, u, lambda blk, idx: blk >= cand[None], b)
            cnt = _count_total(acc)
            take = jnp.logical_and(last, cnt >= k_sel[:, lanes])
            new_blocks.append((jnp.where(take, cand_key, t_key),
                               jnp.where(last, jnp.zeros_like(acc), acc),
                               jnp.where(take, cnt, n_ge)))
        return jnp.where(last, p + 1, p), jnp.where(last, 0, u + 1), tuple(new_blocks)

    def chunk(c, masked, state):
        k0 = pl.multiple_of(c * TKA, TKA)
        box = [state]

        def filler(phase):
            for _ in range(SEL_FILL[phase]):
                box[0] = select_step(box[0])

        if masked:
            causal = (k0 + lax.broadcasted_iota(i32, (TKA, TQ), 0)
                      <= q0 + lax.broadcasted_iota(i32, (TKA, TQ), 1))

        def score_fn(j, r, rows):
            h = j // 2
            s = jnp.dot(kd_ref[pl.ds(k0 + r, rows), h * LANES:(h + 1) * LANES], qz_sc[j],
                        preferred_element_type=f32)
            return jnp.where(causal[r:r + rows], s, NEG) if masked else s

        def value_fn(j):
            h = j // 2
            return vdt_ref[h * LANES:(h + 1) * LANES, pl.ds(k0, TKA)]

        _softmax_chunk(n_sm, score_fn, value_fn, s_sc, p_sc, a_sc, m_sc, acc_sc, filler)
        return box[0]

    assert n_sm * sum(SEL_FILL) == SEL_PASSES
    n_causal = (qpos + 1).astype(f32)
    state = (jnp.int32(0), jnp.int32(0),
             tuple((jnp.full((SUBLANES, LANES), -(2 ** 31), i32), jnp.zeros((SUBLANES, LANES), f32),
                    n_causal[:, lanes]) for lanes in lane_blocks))
    state = lax.fori_loop(0, n_units - 1, lambda c, st: chunk(c, False, st), state)
    _, _, blocks = chunk(n_units - 1, True, state)
    thr = _key_to_float(jnp.concatenate([blk[0] for blk in blocks], axis=1))
    n_ge = jnp.concatenate([blk[2] for blk in blocks], axis=1)

    def count(pred):
        acc = lax.fori_loop(0, n_units, lambda u, acc: acc + _unit_count(sc_sc, u, pred),
                            jnp.zeros((SUBLANES, TQ), f32))
        return _count_total(acc)

    @pl.when(jnp.max(n_ge - k_sel) > 0.0)
    def _():
        need = k_sel - count(lambda blk, idx: blk > thr[None])
        idx_bits = (sc_sc.shape[1] - 1).bit_length()

        def idx_body(b, j):
            cand = j + lax.shift_left(jnp.int32(1), jnp.int32(idx_bits - 1) - b)
            cnt = count(lambda blk, idx: (blk == thr[None]) & (idx < cand[None]))
            return jnp.where(cnt < need, cand, j)

        j_last = lax.fori_loop(0, idx_bits, idx_body, jnp.zeros((SUBLANES, TQ), i32))

        def fix_body(u, carry):
            base = pl.multiple_of(u * TKA, TKA)
            blk = _sc_rows(sc_sc, base, TKA)
            idx = base + lax.broadcasted_iota(i32, blk.shape, 0)
            drop = (blk == thr[0:1]) & (idx > j_last[0:1])
            _sc_store(sc_sc, base, TKA, jnp.where(drop, neg_inf, blk))
            return carry

        lax.fori_loop(0, n_units, fix_body, 0)

    def bias_body(u, carry):
        base = pl.multiple_of(u * TKA, TKA)
        keep = _sc_rows(sc_sc, base, TKA) >= thr[0:1]
        bias_ref[pl.ds(base, TKA), :] = jnp.where(keep, 0.0, NEG).astype(bias_ref.dtype)
        return carry

    def fill_body(u, carry):
        base = pl.multiple_of(u * TKA, TKA)
        bias_ref[pl.ds(base, TKA), :] = jnp.full((TKA, TQ), NEG, bias_ref.dtype)
        return carry

    lax.fori_loop(0, n_units, bias_body, 0)
    lax.fori_loop(n_units, sc_sc.shape[1] // TKA, fill_body, 0)

    lam = (jnp.exp(jnp.sum(lq1_ref[...] * lk1_ref[...], axis=1, keepdims=True))
           - jnp.exp(jnp.sum(lq2_ref[...] * lk2_ref[...], axis=1, keepdims=True)) + lam_init)
    heads = []
    for h in range(DIFF_HEADS):
        o = (_normalized(acc_sc, 2 * h, 2 * DIFF_DIM)
             - lam * _normalized(acc_sc, 2 * h + 1, 2 * DIFF_DIM))
        heads.append(o * lax.rsqrt(jnp.mean(o * o, axis=0, keepdims=True) + EPS))
    o = jnp.concatenate(heads, axis=0).T
    g = g_ref[...]
    od_ref[...] = (o * jnp.concatenate([g] * DIFF_HEADS, axis=1) * (1.0 - lam_init)).astype(od_ref.dtype)


def _seldiff(lq1, lk1, lq2, lk2, g, qit, wit, ki, qdt, kd, vdt, lam_init, batch, seq):
    assert TKA == 2 * TQ
    nq = seq // TQ
    const = lambda b, i: (0, 0)
    qcol = lambda b, i: (0, b * nq + i)
    resident = dict(pipeline_mode=pl.Buffered(1))
    n_sm = 2 * DIFF_HEADS
    vec = pl.BlockSpec((1, DIFF_DIM), const)
    return pl.pallas_call(
        functools.partial(_seldiff_kernel, lam_init=lam_init),
        grid=(batch, nq),
        in_specs=[
            vec, vec, vec, vec,
            pl.BlockSpec((1, 2 * DIFF_DIM), const),
            pl.BlockSpec((GROUP, TQ), qcol),
            pl.BlockSpec((WI_ROWS, TQ), qcol),
            pl.BlockSpec((seq, LANES), lambda b, i: (b, 0), **resident),
            pl.BlockSpec((GROUP, TQ), qcol),
            pl.BlockSpec((seq, GROUP), lambda b, i: (b, 0), **resident),
            pl.BlockSpec((GROUP, seq), lambda b, i: (0, b), **resident),
        ],
        out_specs=(pl.BlockSpec((TQ, GROUP), lambda b, i: (b * nq + i, 0)),
                   pl.BlockSpec((seq, TQ), qcol)),
        out_shape=(jax.ShapeDtypeStruct((batch * seq, GROUP), bf16),
                   jax.ShapeDtypeStruct((seq, batch * seq), bf16)),
        scratch_shapes=[
            pltpu.VMEM((TQ // LANES, seq, LANES), f32),
            pltpu.VMEM((IDX_HEADS, LANES, TQ), bf16),
            pltpu.VMEM((n_sm, LANES, TQ), bf16),
        ] + _softmax_scratch(n_sm, 2 * DIFF_DIM),
        compiler_params=pltpu.CompilerParams(
            dimension_semantics=("arbitrary", "arbitrary"), vmem_limit_bytes=56 * MIB),
        name="seldiff",
    )(lq1, lk1, lq2, lk2, g, qit, wit, ki, qdt, kd, vdt)


def _dsa_kernel(qat_ref, bias_ref, ka_ref, vat_ref, o_ref,
                bias_sc, qza_sc, s_sc, p_sc, a_sc, m_sc, acc_sc):
    q0 = pl.program_id(1) * TQ
    n_units = (q0 + TQ + TKA - 1) // TKA
    _split_pairs_t(qat_ref, qza_sc, DSA_HEADS // 2)
    _init_softmax(DSA_HEADS, m_sc, acc_sc)

    def attn_body(c, carry):
        k0 = pl.multiple_of(c * TKA, TKA)
        bias_sc[...] = bias_ref[pl.ds(k0, TKA), :].astype(f32)

        def score_fn(h, r, rows):
            pair = slice((h // 2) * LANES, (h // 2 + 1) * LANES)
            s = jnp.dot(ka_ref[pl.ds(k0 + r, rows), pair], qza_sc[h], preferred_element_type=f32)
            return s + bias_sc[r:r + rows, :]

        def value_fn(h):
            return vat_ref[h * HEAD_DIM:(h + 1) * HEAD_DIM, pl.ds(k0, TKA)]

        _softmax_chunk(DSA_HEADS, score_fn, value_fn, s_sc, p_sc, a_sc, m_sc, acc_sc, s_rows=DSA_S_ROWS)
        return carry

    lax.fori_loop(0, n_units, attn_body, 0)

    ot = jnp.concatenate([_normalized(acc_sc, h, HEAD_DIM) for h in range(DSA_HEADS)], axis=0)
    o_ref[...] = ot.T.astype(o_ref.dtype)


def _dsa(qat, bias, ka, vat, batch, seq):
    nq = seq // TQ
    qcol = lambda b, i: (0, b * nq + i)
    resident = dict(pipeline_mode=pl.Buffered(1))
    return pl.pallas_call(
        _dsa_kernel,
        grid=(batch, nq),
        in_specs=[
            pl.BlockSpec((GROUP, TQ), qcol),
            pl.BlockSpec((seq, TQ), qcol),
            pl.BlockSpec((seq, GROUP), lambda b, i: (b, 0), **resident),
            pl.BlockSpec((GROUP, seq), lambda b, i: (0, b), **resident),
        ],
        out_specs=pl.BlockSpec((TQ, GROUP), lambda b, i: (b * nq + i, 0)),
        out_shape=jax.ShapeDtypeStruct((batch * seq, GROUP), bf16),
        scratch_shapes=[
            pltpu.VMEM((TKA, TQ), f32),
            pltpu.VMEM((DSA_HEADS, LANES, TQ), bf16),
        ] + _softmax_scratch(DSA_HEADS, HEAD_DIM),
        compiler_params=pltpu.CompilerParams(
            dimension_semantics=("arbitrary", "arbitrary"), vmem_limit_bytes=48 * MIB),
        name="dsa",
    )(qat, bias, ka, vat)


def _post_kernel(x_ref, oa_ref, od_ref, wo_ref, wu_ref, wd_ref, g1_ref, g2_ref, g3_ref, o_ref):
    y = (jnp.dot(oa_ref[...], wo_ref[0:DSA_W, :], preferred_element_type=f32)
         + jnp.dot(od_ref[...], wo_ref[DSA_W:DSA_W + DIFF_W, :], preferred_element_type=f32))
    x1 = x_ref[...] + _rms(y) * g1_ref[...]
    h = (_rms(x1) * g2_ref[...]).astype(bf16)
    acc = jnp.zeros_like(x1)
    for c in range(D_FF // FF_CHUNK):
        cols = slice(c * FF_CHUNK, (c + 1) * FF_CHUNK)
        u = jnp.maximum(jnp.dot(h, wu_ref[:, cols], preferred_element_type=f32), 0.0)
        acc = acc + jnp.dot((u * u).astype(bf16), wd_ref[cols, :], preferred_element_type=f32)
    o_ref[...] = x1 + _rms(acc) * g3_ref[...]


def _post(x2d, oa, od, wo, wu, wd, g1, g2, g3):
    m = x2d.shape[0]
    tm = PROJ_TM
    row = lambda i: (i, 0)
    const = lambda i: (0, 0)
    resident = dict(pipeline_mode=pl.Buffered(1))
    gspec = pl.BlockSpec((1, D_MODEL), const)
    return pl.pallas_call(
        _post_kernel,
        grid=(m // tm,),
        in_specs=[
            pl.BlockSpec((tm, D_MODEL), row),
            pl.BlockSpec((tm, DSA_W), row),
            pl.BlockSpec((tm, DIFF_W), row),
            pl.BlockSpec((DSA_W + DIFF_W, D_MODEL), const, **resident),
            pl.BlockSpec((D_MODEL, D_FF), const, **resident),
            pl.BlockSpec((D_FF, D_MODEL), const, **resident),
            gspec, gspec, gspec,
        ],
        out_specs=pl.BlockSpec((tm, D_MODEL), row),
        out_shape=jax.ShapeDtypeStruct((m, D_MODEL), f32),
        compiler_params=pltpu.CompilerParams(
            dimension_semantics=("arbitrary",), vmem_limit_bytes=56 * MIB),
        name="post",
    )(x2d, oa, od, wo, wu, wd, g1, g2, g3)


def _rope_tables(seq):
    inv = 1.0 / (ROPE_THETA ** (jnp.arange(0, HEAD_DIM, 2, dtype=f32) / HEAD_DIM))
    ang = jnp.arange(seq, dtype=f32)[:, None] * inv[None, :]
    cos, sin = jnp.cos(ang), jnp.sin(ang)
    cs = jnp.concatenate([cos, cos, cos, cos], axis=1)
    sn = jnp.concatenate([-sin, sin, -sin, sin], axis=1)
    return cs, sn, cos.T, sin.T


def _prep_w_in(w):
    q_a, k_a, v_a, q_i, k_i, w_i, q_d, k_d, v_d = jnp.split(w, SPLIT_POINTS, axis=1)
    wn = jnp.concatenate([k_a, k_d, k_i, k_i], axis=1)
    pad = jnp.zeros((D_MODEL, WI_ROWS - IDX_HEADS), w.dtype)
    wt = jnp.concatenate([q_a * SM_SCALE, q_i, q_d * SM_SCALE, v_a, v_d, w_i, pad], axis=1).T
    return wn.astype(bf16), wt.astype(bf16)


def kernel(x, w_in, w_out, w_up, w_down, g_pre_mix, g_post_mix, g_pre_mlp, g_post_mlp,
           g_diff_sub, lambda_q1, lambda_k1, lambda_q2, lambda_k2):
    b, s, d = x.shape
    depth = w_in.shape[0]
    assert d == D_MODEL and s % TKA == 0 and s % PROJ_TM == 0 and min(TOPK_MAX, s // 4) == TOPK_MAX
    cs, sn, cst, snt = _rope_tables(s)
    x2d = x.reshape(b * s, d)
    for layer in range(depth):
        wn, wt = _prep_w_in(w_in[layer])
        ka, kd, ki, qat, qit, qdt, vat, vdt, wit = _proj(
            x2d, g_pre_mix[layer][None, :], cs, sn, cst, snt, wn, wt, s)
        lam_init = 0.8 - 0.6 * math.exp(-0.3 * layer)
        od, bias = _seldiff(lambda_q1[layer][None, :], lambda_k1[layer][None, :],
                            lambda_q2[layer][None, :], lambda_k2[layer][None, :],
                            g_diff_sub[layer][None, :], qit, wit, ki, qdt, kd, vdt, lam_init, b, s)
        oa = _dsa(qat, bias, ka, vat, b, s)
        x2d = _post(x2d, oa, od,
                    w_out[layer].astype(bf16), w_up[layer].astype(bf16), w_down[layer].astype(bf16),
                    g_post_mix[layer][None, :], g_pre_mlp[layer][None, :], g_post_mlp[layer][None, :])
    return x2d.reshape(b, s, d)
```

```python
import functools
import math

import jax
import jax.numpy as jnp
import numpy as np
from jax import lax
from jax.experimental import pallas as pl
from jax.experimental.pallas import tpu as pltpu

f32 = jnp.float32
bf16 = jnp.bfloat16
i32 = jnp.int32

D_MODEL = 1024
HEAD_DIM = 64
DSA_HEADS = 8
DIFF_HEADS = 4
DIFF_DIM = 64
IDX_HEADS = 8
IDX_DIM = 64
TOPK_MAX = 256
D_FF = 4 * D_MODEL
ROPE_THETA = 10000.0
EPS = 1e-6
NEG = -1e30

DSA_W = DSA_HEADS * HEAD_DIM
DIFF_W = DIFF_HEADS * 2 * DIFF_DIM
IN_SIZES = (DSA_W, DSA_W, DSA_W, IDX_HEADS * IDX_DIM, IDX_DIM, IDX_HEADS, DIFF_W, DIFF_W, DIFF_W)
SPLIT_POINTS = tuple(int(v) for v in np.cumsum(IN_SIZES)[:-1])
HALF = HEAD_DIM // 2

LANES = 128
SUBLANES = 8
BF16_ROWS = 16
MIB = 1024 * 1024

GROUP = 512
PROJ_TM = 512
WI_ROWS = BF16_ROWS
T_ROWS = 5 * GROUP + WI_ROWS
N_COLS = 2 * GROUP + LANES
FF_CHUNK = 1024

TQ = 256
TKI = 512
TKA = 512
SEL_PASSES = 32
SEL_FILL = (2, 1, 1)
CNT_ROWS = 32
CNT_ACC = 8
DSA_AHEAD = 4
DSA_AHEAD_DELAY = 2
DSA_S_ROWS = 256
SUM_ROWS = BF16_ROWS
SM_SCALE = HEAD_DIM ** -0.5 * math.log2(math.e)

_NT = (((1,), (1,)), ((), ()))


def _rms(x):
    return x * lax.rsqrt(jnp.mean(x * x, axis=-1, keepdims=True) + EPS)


def _proj_kernel(x_ref, g_ref, cs_ref, sn_ref, cst_ref, snt_ref, wn_ref, wt_ref,
                 ka_ref, kd_ref, ki_ref, qat_ref, qit_ref, qdt_ref, vat_ref, vdt_ref, wit_ref):
    h = (_rms(x_ref[...]) * g_ref[...]).astype(bf16)
    tm = h.shape[0]

    cs = cs_ref[...]
    sn = sn_ref[...]

    def rope_rows(col0, width, out_ref):
        y = jnp.dot(h, wn_ref[:, col0:col0 + width], preferred_element_type=f32)
        lane = lax.broadcasted_iota(i32, (tm, width), 1)
        partner = jnp.where((lane & (HEAD_DIM - 1)) < HALF,
                            pltpu.roll(y, width - HALF, 1), pltpu.roll(y, HALF, 1))
        reps = width // LANES
        c = cs if reps == 1 else jnp.concatenate([cs] * reps, axis=1)
        s = sn if reps == 1 else jnp.concatenate([sn] * reps, axis=1)
        out_ref[...] = (y * c + partner * s).astype(out_ref.dtype)

    rope_rows(0, GROUP, ka_ref)
    rope_rows(GROUP, GROUP, kd_ref)
    rope_rows(2 * GROUP, LANES, ki_ref)

    yt = lax.dot_general(wt_ref[...], h, _NT, preferred_element_type=f32)
    ct = cst_ref[...]
    st = snt_ref[...]

    def rope_cols(row0, out_ref):
        for hd in range(GROUP // HEAD_DIM):
            r = row0 + hd * HEAD_DIM
            x1 = yt[r:r + HALF]
            x2 = yt[r + HALF:r + HEAD_DIM]
            out_ref[hd * HEAD_DIM:hd * HEAD_DIM + HALF, :] = (x1 * ct - x2 * st).astype(out_ref.dtype)
            out_ref[hd * HEAD_DIM + HALF:(hd + 1) * HEAD_DIM, :] = (x2 * ct + x1 * st).astype(out_ref.dtype)

    rope_cols(0, qat_ref)
    rope_cols(GROUP, qit_ref)
    rope_cols(2 * GROUP, qdt_ref)
    vat_ref[...] = yt[3 * GROUP:4 * GROUP].astype(bf16)
    vdt_ref[...] = yt[4 * GROUP:5 * GROUP].astype(bf16)
    idx_scale = (IDX_HEADS ** -0.5) * (IDX_DIM ** -0.5)
    wit_ref[...] = yt[5 * GROUP:5 * GROUP + WI_ROWS] * idx_scale


def _proj(x2d, g, cs, sn, cst, snt, wn, wt, seq):
    m = x2d.shape[0]
    tm = PROJ_TM
    pos_blocks = seq // tm
    row = lambda i: (i, 0)
    col = lambda i: (0, i)
    const = lambda i: (0, 0)
    rows_out = jax.ShapeDtypeStruct((m, GROUP), bf16)
    cols_out = jax.ShapeDtypeStruct((GROUP, m), bf16)
    out_shape = (rows_out, rows_out, jax.ShapeDtypeStruct((m, LANES), bf16),
                 cols_out, cols_out, cols_out, cols_out, cols_out,
                 jax.ShapeDtypeStruct((WI_ROWS, m), f32))
    rows_spec = pl.BlockSpec((tm, GROUP), row)
    cols_spec = pl.BlockSpec((GROUP, tm), col)
    return pl.pallas_call(
        _proj_kernel,
        grid=(m // tm,),
        in_specs=[
            pl.BlockSpec((tm, D_MODEL), row),
            pl.BlockSpec((1, D_MODEL), const),
            pl.BlockSpec((tm, LANES), lambda i: (i % pos_blocks, 0)),
            pl.BlockSpec((tm, LANES), lambda i: (i % pos_blocks, 0)),
            pl.BlockSpec((HALF, tm), lambda i: (0, i % pos_blocks)),
            pl.BlockSpec((HALF, tm), lambda i: (0, i % pos_blocks)),
            pl.BlockSpec((D_MODEL, N_COLS), const),
            pl.BlockSpec((T_ROWS, D_MODEL), const),
        ],
        out_specs=(rows_spec, rows_spec, pl.BlockSpec((tm, LANES), row),
                   cols_spec, cols_spec, cols_spec, cols_spec, cols_spec,
                   pl.BlockSpec((WI_ROWS, tm), col)),
        out_shape=out_shape,
        compiler_params=pltpu.CompilerParams(
            dimension_semantics=("arbitrary",), vmem_limit_bytes=56 * MIB),
        name="proj",
    )(x2d, g, cs, sn, cst, snt, wn, wt)


def _split_pairs_t(src_ref, dst_sc, n_pairs):
    zero = jnp.zeros((HEAD_DIM, dst_sc.shape[2]), dst_sc.dtype)
    for p in range(n_pairs):
        r = p * LANES
        dst_sc[2 * p, 0:HEAD_DIM, :] = src_ref[r:r + HEAD_DIM, :]
        dst_sc[2 * p, HEAD_DIM:LANES, :] = zero
        dst_sc[2 * p + 1, 0:HEAD_DIM, :] = zero
        dst_sc[2 * p + 1, HEAD_DIM:LANES, :] = src_ref[r + HEAD_DIM:r + LANES, :]


def _score_tile(j, score_fn, s_sc, a_sc, s_rows):
    mx = None
    for r in range(0, TKA, s_rows):
        s = score_fn(j, r, s_rows)
        s_sc[j, r:r + s_rows, :] = s
        bm = jnp.max(s, axis=0, keepdims=True)
        mx = bm if mx is None else jnp.maximum(mx, bm)
    a_sc[j] = mx


def _softmax_chunk(n, score_fn, value_fn, s_sc, p_sc, a_sc, m_sc, acc_sc, filler=None, s_rows=TKA,
                   scored=0):
    ones = jnp.ones((SUM_ROWS, TKA), bf16)
    fill = filler if filler is not None else (lambda phase: None)

    def scores(j):
        _score_tile(j, score_fn, s_sc, a_sc, s_rows)
        fill(0)

    def softmax(j):
        m_old = m_sc[j]
        m_new = jnp.maximum(m_old, a_sc[j])
        p_sc[j] = jnp.exp2(s_sc[j] - m_new).astype(bf16)
        a_sc[j] = jnp.exp2(m_old - m_new)
        m_sc[j] = m_new
        fill(1)

    def values(j):
        vt = jnp.concatenate([value_fn(j), ones], axis=0)
        acc_sc[j] = a_sc[j] * acc_sc[j] + jnp.dot(vt, p_sc[j], preferred_element_type=f32)
        fill(2)

    for j in range(scored, n):
        scores(j)
    for j in range(n):
        softmax(j)
    for j in range(n):
        values(j)


def _init_softmax(n, m_sc, acc_sc):
    for j in range(n):
        m_sc[j] = jnp.full(m_sc.shape[1:], NEG, f32)
        acc_sc[j] = jnp.zeros(acc_sc.shape[1:], f32)


def _normalized(acc_sc, j, dv):
    return acc_sc[j, 0:dv, :] / acc_sc[j, dv:dv + 1, :]


def _softmax_scratch(n, dv):
    return [
        pltpu.VMEM((n, TKA, TQ), f32),
        pltpu.VMEM((n, TKA, TQ), bf16),
        pltpu.VMEM((n, 1, TQ), f32),
        pltpu.VMEM((n, 1, TQ), f32),
        pltpu.VMEM((n, dv + SUM_ROWS, TQ), f32),
    ]


def _key_to_float(key):
    return pltpu.bitcast(key ^ ((key >> 31) & jnp.int32(0x7FFFFFFF)), f32)


def _sc_rows(sc_sc, base, rows, block=None):
    if block is not None:
        return sc_sc[block, pl.ds(base, rows), :]
    return jnp.concatenate([sc_sc[b, pl.ds(base, rows), :] for b in range(sc_sc.shape[0])], axis=1)


def _sc_store(sc_sc, base, rows, val):
    for b in range(sc_sc.shape[0]):
        sc_sc[b, pl.ds(base, rows), :] = val[:, b * LANES:(b + 1) * LANES]


def _unit_count(sc_sc, u, pred, block=None):
    base = pl.multiple_of(u * TKA, TKA)
    width = LANES if block is not None else sc_sc.shape[0] * LANES
    shape = (CNT_ROWS // SUBLANES, SUBLANES, width)
    sub = (lax.broadcasted_iota(i32, shape, 0) * SUBLANES + lax.broadcasted_iota(i32, shape, 1))
    parts = [jnp.zeros((SUBLANES, width), f32) for _ in range(CNT_ACC)]
    for j in range(TKA // CNT_ROWS):
        blk = _sc_rows(sc_sc, base + j * CNT_ROWS, CNT_ROWS, block).reshape(shape)
        ind = jnp.where(pred(blk, base + j * CNT_ROWS + sub), 1.0, 0.0).astype(f32)
        parts[j % CNT_ACC] = parts[j % CNT_ACC] + jnp.sum(ind, axis=0)
    return functools.reduce(lambda a, b: a + b, parts)


def _count_total(acc):
    return jnp.broadcast_to(jnp.sum(acc, axis=0, keepdims=True), acc.shape)


def _seldiff_kernel(lq1_ref, lk1_ref, lq2_ref, lk2_ref, g_ref, qit_ref, wit_ref, ki_ref,
                    qdt_ref, kd_ref, vdt_ref, od_ref, bias_ref,
                    sc_sc, qzi_sc, qz_sc, s_sc, p_sc, a_sc, m_sc, acc_sc, *, lam_init):
    q0 = pl.program_id(1) * TQ
    n_units = (q0 + TQ + TKA - 1) // TKA
    n_sm = 2 * DIFF_HEADS
    neg_inf = jnp.float32(-jnp.inf)

    _split_pairs_t(qit_ref, qzi_sc, IDX_HEADS // 2)
    _split_pairs_t(qdt_ref, qz_sc, DIFF_HEADS)
    _init_softmax(n_sm, m_sc, acc_sc)

    key_a = lax.broadcasted_iota(i32, (TKI, TQ), 0)
    qry_a = q0 + lax.broadcasted_iota(i32, (TKI, TQ), 1)

    def score_body(c, carry):
        k0 = pl.multiple_of(c * TKI, TKI)
        kc = ki_ref[pl.ds(k0, TKI), :]
        score = jnp.zeros((TKI, TQ), f32)
        for h in range(IDX_HEADS):
            logit = jnp.dot(kc, qzi_sc[h], preferred_element_type=f32)
            score = score + wit_ref[h:h + 1, :] * jnp.maximum(logit, 0.0)
        _sc_store(sc_sc, k0, TKI, jnp.where(key_a + k0 <= qry_a, score, neg_inf))
        return carry

    lax.fori_loop(0, n_units * (TKA // TKI), score_body, 0)

    qpos = q0 + lax.broadcasted_iota(i32, (SUBLANES, TQ), 1)
    k_sel = jnp.minimum(qpos + 1, TOPK_MAX).astype(f32)

    lane_blocks = [slice(b * LANES, (b + 1) * LANES) for b in range(TQ // LANES)]

    def select_step(state):
        p, u, blocks = state
        bit = lax.shift_left(jnp.int32(1), jnp.int32(31) - p)
        last = u == n_units - 1
        new_blocks = []
        for b, (lanes, (t_key, acc, n_ge)) in enumerate(zip(lane_blocks, blocks)):
            cand_key = t_key + bit
            cand = _key_to_float(cand_key)
            acc = acc + _unit_count(sc_sc, u, lambda blk, idx: blk >= cand[None], b)
            cnt = _count_total(acc)
            take = jnp.logical_and(last, cnt >= k_sel[:, lanes])
            new_blocks.append((jnp.where(take, cand_key, t_key),
                               jnp.where(last, jnp.zeros_like(acc), acc),
                               jnp.where(take, cnt, n_ge)))
        return jnp.where(last, p + 1, p), jnp.where(last, 0, u + 1), tuple(new_blocks)

    def chunk(c, masked, state):
        k0 = pl.multiple_of(c * TKA, TKA)
        box = [state]

        def filler(phase):
            for _ in range(SEL_FILL[phase]):
                box[0] = select_step(box[0])

        if masked:
            causal = (k0 + lax.broadcasted_iota(i32, (TKA, TQ), 0)
                      <= q0 + lax.broadcasted_iota(i32, (TKA, TQ), 1))

        def score_fn(j, r, rows):
            h = j // 2
            s = jnp.dot(kd_ref[pl.ds(k0 + r, rows), h * LANES:(h + 1) * LANES], qz_sc[j],
                        preferred_element_type=f32)
            return jnp.where(causal[r:r + rows], s, NEG) if masked else s

        def value_fn(j):
            h = j // 2
            return vdt_ref[h * LANES:(h + 1) * LANES, pl.ds(k0, TKA)]

        _softmax_chunk(n_sm, score_fn, value_fn, s_sc, p_sc, a_sc, m_sc, acc_sc, filler)
        return box[0]

    assert n_sm * sum(SEL_FILL) == SEL_PASSES
    n_causal = (qpos + 1).astype(f32)
    state = (jnp.int32(0), jnp.int32(0),
             tuple((jnp.full((SUBLANES, LANES), -(2 ** 31), i32), jnp.zeros((SUBLANES, LANES), f32),
                    n_causal[:, lanes]) for lanes in lane_blocks))
    state = lax.fori_loop(0, n_units - 1, lambda c, st: chunk(c, False, st), state)
    _, _, blocks = chunk(n_units - 1, True, state)
    thr = _key_to_float(jnp.concatenate([blk[0] for blk in blocks], axis=1))
    n_ge = jnp.concatenate([blk[2] for blk in blocks], axis=1)

    def count(pred):
        acc = lax.fori_loop(0, n_units, lambda u, acc: acc + _unit_count(sc_sc, u, pred),
                            jnp.zeros((SUBLANES, TQ), f32))
        return _count_total(acc)

    @pl.when(jnp.max(n_ge - k_sel) > 0.0)
    def _():
        need = k_sel - count(lambda blk, idx: blk > thr[None])
        idx_bits = (sc_sc.shape[1] - 1).bit_length()

        def idx_body(b, j):
            cand = j + lax.shift_left(jnp.int32(1), jnp.int32(idx_bits - 1) - b)
            cnt = count(lambda blk, idx: (blk == thr[None]) & (idx < cand[None]))
            return jnp.where(cnt < need, cand, j)

        j_last = lax.fori_loop(0, idx_bits, idx_body, jnp.zeros((SUBLANES, TQ), i32))

        def fix_body(u, carry):
            base = pl.multiple_of(u * TKA, TKA)
            blk = _sc_rows(sc_sc, base, TKA)
            idx = base + lax.broadcasted_iota(i32, blk.shape, 0)
            drop = (blk == thr[0:1]) & (idx > j_last[0:1])
            _sc_store(sc_sc, base, TKA, jnp.where(drop, neg_inf, blk))
            return carry

        lax.fori_loop(0, n_units, fix_body, 0)

    def bias_body(u, carry):
        base = pl.multiple_of(u * TKA, TKA)
        keep = _sc_rows(sc_sc, base, TKA) >= thr[0:1]
        bias_ref[pl.ds(base, TKA), :] = jnp.where(keep, 0.0, NEG).astype(bias_ref.dtype)
        return carry

    def fill_body(u, carry):
        base = pl.multiple_of(u * TKA, TKA)
        bias_ref[pl.ds(base, TKA), :] = jnp.full((TKA, TQ), NEG, bias_ref.dtype)
        return carry

    lax.fori_loop(0, n_units, bias_body, 0)
    lax.fori_loop(n_units, sc_sc.shape[1] // TKA, fill_body, 0)

    lam = (jnp.exp(jnp.sum(lq1_ref[...] * lk1_ref[...], axis=1, keepdims=True))
           - jnp.exp(jnp.sum(lq2_ref[...] * lk2_ref[...], axis=1, keepdims=True)) + lam_init)
    heads = []
    for h in range(DIFF_HEADS):
        o = (_normalized(acc_sc, 2 * h, 2 * DIFF_DIM)
             - lam * _normalized(acc_sc, 2 * h + 1, 2 * DIFF_DIM))
        heads.append(o * lax.rsqrt(jnp.mean(o * o, axis=0, keepdims=True) + EPS))
    o = jnp.concatenate(heads, axis=0).T
    g = g_ref[...]
    od_ref[...] = (o * jnp.concatenate([g] * DIFF_HEADS, axis=1) * (1.0 - lam_init)).astype(od_ref.dtype)


def _seldiff(lq1, lk1, lq2, lk2, g, qit, wit, ki, qdt, kd, vdt, lam_init, batch, seq):
    assert TKA == 2 * TQ
    nq = seq // TQ
    const = lambda b, i: (0, 0)
    qcol = lambda b, i: (0, b * nq + i)
    resident = dict(pipeline_mode=pl.Buffered(1))
    n_sm = 2 * DIFF_HEADS
    vec = pl.BlockSpec((1, DIFF_DIM), const)
    return pl.pallas_call(
        functools.partial(_seldiff_kernel, lam_init=lam_init),
        grid=(batch, nq),
        in_specs=[
            vec, vec, vec, vec,
            pl.BlockSpec((1, 2 * DIFF_DIM), const),
            pl.BlockSpec((GROUP, TQ), qcol),
            pl.BlockSpec((WI_ROWS, TQ), qcol),
            pl.BlockSpec((seq, LANES), lambda b, i: (b, 0), **resident),
            pl.BlockSpec((GROUP, TQ), qcol),
            pl.BlockSpec((seq, GROUP), lambda b, i: (b, 0), **resident),
            pl.BlockSpec((GROUP, seq), lambda b, i: (0, b), **resident),
        ],
        out_specs=(pl.BlockSpec((TQ, GROUP), lambda b, i: (b * nq + i, 0)),
                   pl.BlockSpec((seq, TQ), qcol)),
        out_shape=(jax.ShapeDtypeStruct((batch * seq, GROUP), bf16),
                   jax.ShapeDtypeStruct((seq, batch * seq), bf16)),
        scratch_shapes=[
            pltpu.VMEM((TQ // LANES, seq, LANES), f32),
            pltpu.VMEM((IDX_HEADS, LANES, TQ), bf16),
            pltpu.VMEM((n_sm, LANES, TQ), bf16),
        ] + _softmax_scratch(n_sm, 2 * DIFF_DIM),
        compiler_params=pltpu.CompilerParams(
            dimension_semantics=("arbitrary", "arbitrary"), vmem_limit_bytes=56 * MIB),
        name="seldiff",
    )(lq1, lk1, lq2, lk2, g, qit, wit, ki, qdt, kd, vdt)


def _dsa_kernel(qat_ref, bias_ref, ka_ref, vat_ref, o_ref,
                bias_sc, qza_sc, s_sc, p_sc, a_sc, m_sc, acc_sc):
    q0 = pl.program_id(1) * TQ
    n_units = (q0 + TQ + TKA - 1) // TKA
    _split_pairs_t(qat_ref, qza_sc, DSA_HEADS // 2)
    _init_softmax(DSA_HEADS, m_sc, acc_sc)

    def load_bias(c):
        k0 = pl.multiple_of(c * TKA, TKA)
        bias_sc[c % 2] = bias_ref[pl.ds(k0, TKA), :].astype(f32)

    def make_score_fn(c):
        k0 = pl.multiple_of(c * TKA, TKA)

        def score_fn(h, r, rows):
            pair = slice((h // 2) * LANES, (h // 2 + 1) * LANES)
            s = jnp.dot(ka_ref[pl.ds(k0 + r, rows), pair], qza_sc[h], preferred_element_type=f32)
            return s + bias_sc[c % 2, r:r + rows, :]
        return score_fn

    load_bias(0)
    for h in range(DSA_AHEAD):
        _score_tile(h, make_score_fn(0), s_sc, a_sc, DSA_S_ROWS)

    def attn_body(c, carry):
        k0 = pl.multiple_of(c * TKA, TKA)
        nxt = jnp.minimum(c + 1, n_units - 1)
        load_bias(nxt)
        next_score_fn = make_score_fn(nxt)
        done = [0]

        def filler(phase):
            if phase == 2:
                done[0] += 1
                t = done[0] - 1 - DSA_AHEAD_DELAY
                if 0 <= t < DSA_AHEAD:
                    _score_tile(t, next_score_fn, s_sc, a_sc, DSA_S_ROWS)

        def value_fn(h):
            return vat_ref[h * HEAD_DIM:(h + 1) * HEAD_DIM, pl.ds(k0, TKA)]

        _softmax_chunk(DSA_HEADS, make_score_fn(c), value_fn, s_sc, p_sc, a_sc, m_sc, acc_sc, filler,
                       s_rows=DSA_S_ROWS, scored=DSA_AHEAD)
        return carry

    lax.fori_loop(0, n_units, attn_body, 0)

    ot = jnp.concatenate([_normalized(acc_sc, h, HEAD_DIM) for h in range(DSA_HEADS)], axis=0)
    o_ref[...] = ot.T.astype(o_ref.dtype)


def _dsa(qat, bias, ka, vat, batch, seq):
    nq = seq // TQ
    qcol = lambda b, i: (0, b * nq + i)
    resident = dict(pipeline_mode=pl.Buffered(1))
    return pl.pallas_call(
        _dsa_kernel,
        grid=(batch, nq),
        in_specs=[
            pl.BlockSpec((GROUP, TQ), qcol),
            pl.BlockSpec((seq, TQ), qcol),
            pl.BlockSpec((seq, GROUP), lambda b, i: (b, 0), **resident),
            pl.BlockSpec((GROUP, seq), lambda b, i: (0, b), **resident),
        ],
        out_specs=pl.BlockSpec((TQ, GROUP), lambda b, i: (b * nq + i, 0)),
        out_shape=jax.ShapeDtypeStruct((batch * seq, GROUP), bf16),
        scratch_shapes=[
            pltpu.VMEM((2, TKA, TQ), f32),
            pltpu.VMEM((DSA_HEADS, LANES, TQ), bf16),
        ] + _softmax_scratch(DSA_HEADS, HEAD_DIM),
        compiler_params=pltpu.CompilerParams(
            dimension_semantics=("arbitrary", "arbitrary"), vmem_limit_bytes=48 * MIB),
        name="dsa",
    )(qat, bias, ka, vat)


def _post_kernel(x_ref, oa_ref, od_ref, wo_ref, wu_ref, wd_ref, g1_ref, g2_ref, g3_ref, o_ref):
    y = (jnp.dot(oa_ref[...], wo_ref[0:DSA_W, :], preferred_element_type=f32)
         + jnp.dot(od_ref[...], wo_ref[DSA_W:DSA_W + DIFF_W, :], preferred_element_type=f32))
    x1 = x_ref[...] + _rms(y) * g1_ref[...]
    h = (_rms(x1) * g2_ref[...]).astype(bf16)
    acc = jnp.zeros_like(x1)
    for c in range(D_FF // FF_CHUNK):
        cols = slice(c * FF_CHUNK, (c + 1) * FF_CHUNK)
        u = jnp.maximum(jnp.dot(h, wu_ref[:, cols], preferred_element_type=f32), 0.0)
        acc = acc + jnp.dot((u * u).astype(bf16), wd_ref[cols, :], preferred_element_type=f32)
    o_ref[...] = x1 + _rms(acc) * g3_ref[...]


def _post(x2d, oa, od, wo, wu, wd, g1, g2, g3):
    m = x2d.shape[0]
    tm = PROJ_TM
    row = lambda i: (i, 0)
    const = lambda i: (0, 0)
    resident = dict(pipeline_mode=pl.Buffered(1))
    gspec = pl.BlockSpec((1, D_MODEL), const)
    return pl.pallas_call(
        _post_kernel,
        grid=(m // tm,),
        in_specs=[
            pl.BlockSpec((tm, D_MODEL), row),
            pl.BlockSpec((tm, DSA_W), row),
            pl.BlockSpec((tm, DIFF_W), row),
            pl.BlockSpec((DSA_W + DIFF_W, D_MODEL), const, **resident),
            pl.BlockSpec((D_MODEL, D_FF), const, **resident),
            pl.BlockSpec((D_FF, D_MODEL), const, **resident),
            gspec, gspec, gspec,
        ],
        out_specs=pl.BlockSpec((tm, D_MODEL), row),
        out_shape=jax.ShapeDtypeStruct((m, D_MODEL), f32),
        compiler_params=pltpu.CompilerParams(
            dimension_semantics=("arbitrary",), vmem_limit_bytes=56 * MIB),
        name="post",
    )(x2d, oa, od, wo, wu, wd, g1, g2, g3)


def _rope_tables(seq):
    inv = 1.0 / (ROPE_THETA ** (jnp.arange(0, HEAD_DIM, 2, dtype=f32) / HEAD_DIM))
    ang = jnp.arange(seq, dtype=f32)[:, None] * inv[None, :]
    cos, sin = jnp.cos(ang), jnp.sin(ang)
    cs = jnp.concatenate([cos, cos, cos, cos], axis=1)
    sn = jnp.concatenate([-sin, sin, -sin, sin], axis=1)
    return cs, sn, cos.T, sin.T


def _prep_w_in(w):
    q_a, k_a, v_a, q_i, k_i, w_i, q_d, k_d, v_d = jnp.split(w, SPLIT_POINTS, axis=1)
    wn = jnp.concatenate([k_a, k_d, k_i, k_i], axis=1)
    pad = jnp.zeros((D_MODEL, WI_ROWS - IDX_HEADS), w.dtype)
    wt = jnp.concatenate([q_a * SM_SCALE, q_i, q_d * SM_SCALE, v_a, v_d, w_i, pad], axis=1).T
    return wn.astype(bf16), wt.astype(bf16)


def kernel(x, w_in, w_out, w_up, w_down, g_pre_mix, g_post_mix, g_pre_mlp, g_post_mlp,
           g_diff_sub, lambda_q1, lambda_k1, lambda_q2, lambda_k2):
    b, s, d = x.shape
    depth = w_in.shape[0]
    assert d == D_MODEL and s % TKA == 0 and s % PROJ_TM == 0 and min(TOPK_MAX, s // 4) == TOPK_MAX
    cs, sn, cst, snt = _rope_tables(s)
    x2d = x.reshape(b * s, d)
    for layer in range(depth):
        wn, wt = _prep_w_in(w_in[layer])
        ka, kd, ki, qat, qit, qdt, vat, vdt, wit = _proj(
            x2d, g_pre_mix[layer][None, :], cs, sn, cst, snt, wn, wt, s)
        lam_init = 0.8 - 0.6 * math.exp(-0.3 * layer)
        od, bias = _seldiff(lambda_q1[layer][None, :], lambda_k1[layer][None, :],
                            lambda_q2[layer][None, :], lambda_k2[layer][None, :],
                            g_diff_sub[layer][None, :], qit, wit, ki, qdt, kd, vdt, lam_init, b, s)
        oa = _dsa(qat, bias, ka, vat, b, s)
        x2d = _post(x2d, oa, od,
                    w_out[layer].astype(bf16), w_up[layer].astype(bf16), w_down[layer].astype(bf16),
                    g_post_mix[layer][None, :], g_pre_mlp[layer][None, :], g_post_mlp[layer][None, :])
    return x2d.reshape(b, s, d)
```
